```python
import math
import jax, jax.numpy as jnp
from jax import lax
import numpy as np

D_MODEL = 1024
BATCH = 8
SEQ = 2048
DEPTH = 2
DEC_BATCH = 128
DEC_SEQ = 1
PAST_LEN = 16384
PAGE_SIZE = 128

CHUNK = 128
D_A = D_MODEL
GMLP_HEADS = 4
GMLP_HEAD_DIM = D_A // GMLP_HEADS
D_B = D_MODEL
SSM_GROUP = 16
SSM_GROUPS = D_B // SSM_GROUP
SSM_STATE = 64
D_FF = 4 * D_MODEL
N_IN = 2 * D_A + D_B + 2 * D_MODEL
EPS = 1e-6

kernel_name = "gmlp_s5_gated_hybrid_step"


def rmsnorm(x, g):
    xf = x.astype(jnp.float32)
    y = xf * lax.rsqrt(jnp.mean(xf * xf, axis=-1, keepdims=True) + EPS)
    return (y * g.astype(jnp.float32)).astype(x.dtype)


def causal_chunk_mix(v, w_s, b_s):
    bsz, L = v.shape[0], v.shape[1]
    lc = min(L, CHUNK)
    n = -(-L // lc)
    pad = n * lc - L
    w = jnp.tril(w_s[:, :lc, :lc])
    vp = jnp.pad(v, ((0, 0), (0, pad), (0, 0), (0, 0)))
    vp = vp.reshape(bsz, n, lc, GMLP_HEADS, GMLP_HEAD_DIM)
    out = jnp.einsum('hts,bcshd->bcthd', w, vp) + jnp.transpose(b_s[:, :lc])[None, None, :, :, None]
    return out.reshape(bsz, n * lc, GMLP_HEADS, GMLP_HEAD_DIM)[:, :L]


def _ssm_combine(left, right):
    a1, b1 = left
    a2, b2 = right
    return a2 * a1, a2 * b1 + b2


def s5_branch(xs, lam_re, lam_im, log_dt, b_re, b_im, c_re, c_im, d_skip, w_glu, b_glu, h0):
    f32 = jnp.float32
    bsz, L = xs.shape[0], xs.shape[1]
    xf = xs.astype(f32).reshape(bsz, L, SSM_GROUPS, SSM_GROUP)
    lam = lax.complex(lam_re.astype(f32), lam_im.astype(f32))
    dt = jnp.exp(log_dt.astype(f32))[:, None]
    a_bar = jnp.exp(lam * dt)
    b_mat = lax.complex(b_re.astype(f32), b_im.astype(f32))
    b_bar = ((a_bar - 1.0) / lam)[:, :, None] * b_mat
    c_mat = lax.complex(c_re.astype(f32), c_im.astype(f32))
    bu = jnp.einsum('blgh,gph->blgp', xf.astype(jnp.complex64), b_bar)
    if h0 is not None:
        bu = bu.at[:, 0].add(a_bar[None] * h0)
    a_full = jnp.broadcast_to(a_bar, bu.shape)
    _, hs = lax.associative_scan(_ssm_combine, (a_full, bu), axis=1)
    y = jnp.einsum('blgp,ghp->blgh', hs, c_mat).real + d_skip.astype(f32).reshape(SSM_GROUPS, SSM_GROUP) * xf
    z = jax.nn.gelu(y.reshape(bsz, L, D_B))
    out = z * jax.nn.sigmoid(z @ w_glu.astype(f32) + b_glu.astype(f32))
    return out.astype(xs.dtype), hs[:, -1]


def trunk(x, c, h0_re, h0_im, p, keep_v):
    bsz, L = x.shape[0], x.shape[1]
    re_list, im_list, v_list = [], [], []
    for l in range(DEPTH):
        mod = (jax.nn.silu(c) @ p['w_ada'][l] + p['b_ada'][l])[:, None, :]
        sh1, sc1, gt1, sh2, sc2, gt2 = jnp.split(mod, 6, axis=-1)
        h = rmsnorm(x, p['g_norm1'][l]) * (1.0 + sc1) + sh1
        proj = h @ p['w_in'][l]
        pu, pv, ps, pga, pgb = jnp.split(
            proj, [D_A, 2 * D_A, 2 * D_A + D_B, 2 * D_A + D_B + D_MODEL], axis=-1)
        u = jax.nn.gelu(pu)
        v = rmsnorm(jax.nn.gelu(pv), p['g_v'][l])
        mixed = causal_chunk_mix(v.reshape(bsz, L, GMLP_HEADS, GMLP_HEAD_DIM),
                                 p['w_spatial'][l], p['b_spatial'][l])
        y_a = u * mixed.reshape(bsz, L, D_A)
        h0 = None if h0_re is None else lax.complex(h0_re[l].astype(jnp.float32),
                                                     h0_im[l].astype(jnp.float32))
        y_b, h_last = s5_branch(ps, p['lam_re'][l], p['lam_im'][l], p['log_dt'][l],
                                p['b_re'][l], p['b_im'][l], p['c_re'][l], p['c_im'][l],
                                p['d_skip'][l], p['w_glu'][l], p['b_glu'][l], h0)
        merged = jax.nn.sigmoid(pga) * y_a + jax.nn.sigmoid(pgb) * y_b
        x = x + gt1 * (merged @ p['w_out'][l])
        h = rmsnorm(x, p['g_norm2'][l]) * (1.0 + sc2) + sh2
        x = x + gt2 * (jnp.square(jax.nn.relu(h @ p['w_ff1'][l])) @ p['w_ff2'][l])
        re_list.append(h_last.real)
        im_list.append(h_last.imag)
        if keep_v:
            v_list.append(v)
    y = rmsnorm(x, p['g_final'])
    v_stack = jnp.stack(v_list, axis=0) if keep_v else None
    return y, jnp.stack(re_list, axis=0), jnp.stack(im_list, axis=0), v_stack


def setup_inputs(seed: int = 0) -> dict:
    key = jax.random.key(seed)
    ks = jax.random.split(key, 32)
    f32 = jnp.float32
    nrm = lambda k, shape, s: jax.random.normal(k, shape, f32) * s
    lam_im = jnp.tile(math.pi * jnp.arange(SSM_STATE, dtype=f32)[None, None, :], (DEPTH, SSM_GROUPS, 1))
    return {
        "x_prompt": nrm(ks[0], (BATCH, SEQ, D_MODEL), 1.0),
        "x_sample": nrm(ks[1], (DEC_BATCH, DEC_SEQ, D_MODEL), 1.0),
        "c_prompt": nrm(ks[2], (BATCH, D_MODEL), 1.0),
        "c_sample": nrm(ks[3], (DEC_BATCH, D_MODEL), 1.0),
        "state_ssm_re": nrm(ks[4], (DEPTH, DEC_BATCH, SSM_GROUPS, SSM_STATE), 0.5),
        "state_ssm_im": nrm(ks[5], (DEPTH, DEC_BATCH, SSM_GROUPS, SSM_STATE), 0.5),
        "w_ada": nrm(ks[6], (DEPTH, D_MODEL, 6 * D_MODEL), 0.5 * D_MODEL ** -0.5),
        "b_ada": nrm(ks[7], (DEPTH, 6 * D_MODEL), 0.01),
        "g_norm1": 1.0 + nrm(ks[8], (DEPTH, D_MODEL), 0.05),
        "g_norm2": 1.0 + nrm(ks[9], (DEPTH, D_MODEL), 0.05),
        "w_in": nrm(ks[10], (DEPTH, D_MODEL, N_IN), D_MODEL ** -0.5),
        "g_v": 1.0 + nrm(ks[11], (DEPTH, D_A), 0.05),
        "w_spatial": nrm(ks[12], (DEPTH, GMLP_HEADS, CHUNK, CHUNK), CHUNK ** -0.5),
        "b_spatial": 1.0 + nrm(ks[13], (DEPTH, GMLP_HEADS, CHUNK), 0.1),
        "lam_re": -0.5 * jnp.exp(nrm(ks[14], (DEPTH, SSM_GROUPS, SSM_STATE), 0.05)),
        "lam_im": lam_im,
        "log_dt": jax.random.uniform(ks[15], (DEPTH, SSM_GROUPS), f32, math.log(1e-3), math.log(1e-1)),
        "b_re": nrm(ks[16], (DEPTH, SSM_GROUPS, SSM_STATE, SSM_GROUP), (2 * SSM_GROUP) ** -0.5),
        "b_im": nrm(ks[17], (DEPTH, SSM_GROUPS, SSM_STATE, SSM_GROUP), (2 * SSM_GROUP) ** -0.5),
        "c_re": nrm(ks[18], (DEPTH, SSM_GROUPS, SSM_GROUP, SSM_STATE), (2 * SSM_STATE) ** -0.5),
        "c_im": nrm(ks[19], (DEPTH, SSM_GROUPS, SSM_GROUP, SSM_STATE), (2 * SSM_STATE) ** -0.5),
        "d_skip": nrm(ks[20], (DEPTH, D_B), 1.0),
        "w_glu": nrm(ks[21], (DEPTH, D_B, D_B), D_B ** -0.5),
        "b_glu": nrm(ks[22], (DEPTH, D_B), 0.01),
        "w_out": nrm(ks[23], (DEPTH, D_MODEL, D_MODEL), D_MODEL ** -0.5),
        "w_ff1": nrm(ks[24], (DEPTH, D_MODEL, D_FF), D_MODEL ** -0.5),
        "w_ff2": nrm(ks[25], (DEPTH, D_FF, D_MODEL), D_FF ** -0.5),
        "g_final": 1.0 + nrm(ks[26], (D_MODEL,), 0.05),
    }


def reference(x_prompt, x_sample, c_prompt, c_sample, state_ssm_re, state_ssm_im,
              w_ada, b_ada, g_norm1, g_norm2, w_in, g_v, w_spatial, b_spatial,
              lam_re, lam_im, log_dt, b_re, b_im, c_re, c_im, d_skip, w_glu, b_glu,
              w_out, w_ff1, w_ff2, g_final):
    p = dict(w_ada=w_ada, b_ada=b_ada, g_norm1=g_norm1, g_norm2=g_norm2, w_in=w_in, g_v=g_v,
             w_spatial=w_spatial, b_spatial=b_spatial, lam_re=lam_re, lam_im=lam_im,
             log_dt=log_dt, b_re=b_re, b_im=b_im, c_re=c_re, c_im=c_im, d_skip=d_skip,
             w_glu=w_glu, b_glu=b_glu, w_out=w_out, w_ff1=w_ff1, w_ff2=w_ff2, g_final=g_final)
    y_prompt, ssm_re_prompt, ssm_im_prompt, _ = trunk(x_prompt, c_prompt, None, None, p, False)
    y_sample, ssm_re_sample, ssm_im_sample, gmlp_v_sample = trunk(
        x_sample, c_sample, state_ssm_re, state_ssm_im, p, True)
    return (y_prompt, y_sample, ssm_re_prompt, ssm_im_prompt, ssm_re_sample, ssm_im_sample, gmlp_v_sample)
```

```python
import functools
import math

import jax
import jax.numpy as jnp
from jax import lax
from jax.experimental import pallas as pl
from jax.experimental.pallas import tpu as pltpu

F32 = jnp.float32
BF16 = jnp.bfloat16

D = 1024
DEPTH = 2
CHUNK = 128
HEADS = 4
HEAD_DIM = D // HEADS
GROUPS = 64
GROUP_W = 16
STATE = 64
D_FF = 4 * D
N_IN = 5 * D
EPS = 1e-6

LANES = 128
SUBLANES = 8
PAIRS = GROUPS // 2
PAIR_W = 4 * STATE
S_W = PAIRS * PAIR_W
SLABS = D // LANES
C_TILES = 4
C_K = S_W // C_TILES

SEQ_TILE = CHUNK
ROW_CHUNK = 256
SUB_T = 16
PITCH = SEQ_TILE + SUBLANES
MLP_ROWS = 512
VMEM_LIMIT = 60 * 1024 * 1024

_GELU_C = math.sqrt(2.0 / math.pi)


def _gelu(x):
    return x * (0.5 * (1.0 + jnp.tanh(_GELU_C * (x + 0.044715 * (x * x * x)))))


def _sigmoid(x):
    return 1.0 / (1.0 + jnp.exp(-x))


def _rms(x, g):
    return x * lax.rsqrt(jnp.mean(x * x, axis=-1, keepdims=True) + EPS) * g


def _bdot(a, b):
    return jnp.dot(a.astype(BF16), b, preferred_element_type=F32)


def _const_spec(shape):
    zeros = (0,) * len(shape)
    return pl.BlockSpec(shape, lambda *_: zeros, pipeline_mode=pl.Buffered(1))


def _s5_prep_kernel(lr_ref, li_ref, ldt_ref, lrw_ref, liw_ref, ldtw_ref, br_ref, bi_ref,
                    ar_ref, ai_ref, bbr_ref, bbi_ref):
    def a_bar(lr, li, ldt):
        dt = jnp.exp(ldt)
        mag = jnp.exp(lr * dt)
        return mag * jnp.cos(li * dt), mag * jnp.sin(li * dt)

    ar, ai = a_bar(lr_ref[...], li_ref[...], ldt_ref[...])
    ar_ref[...] = ar
    ai_ref[...] = ai
    lr, li = lrw_ref[...], liw_ref[...]
    ar, ai = a_bar(lr, li, ldtw_ref[...])
    nr, ni = ar - 1.0, ai
    den = lr * lr + li * li
    cr = (nr * lr + ni * li) / den
    ci = (ni * lr - nr * li) / den
    br, bi = br_ref[...], bi_ref[...]
    bbr_ref[...] = cr * br - ci * bi
    bbi_ref[...] = cr * bi + ci * br


def _s5_prep(lam_re, lam_im, log_dt, b_re, b_im):
    wide = (DEPTH, GROUPS, STATE * GROUP_W)
    ldt = jnp.broadcast_to(log_dt[:, :, None], lam_re.shape)
    rep = lambda a: jnp.repeat(a, GROUP_W, axis=-1)
    small = jax.ShapeDtypeStruct(lam_re.shape, F32)
    big = jax.ShapeDtypeStruct(wide, F32)
    return pl.pallas_call(
        _s5_prep_kernel, out_shape=(small, small, big, big), name="s5_prep",
    )(lam_re, lam_im, ldt, rep(lam_re), rep(lam_im), rep(ldt),
      b_re.reshape(wide), b_im.reshape(wide))


def _s5_matrices(a_re, a_im, bb_re, bb_im, c_re, c_im):
    a8 = lambda a: jnp.broadcast_to(a.reshape(1, PAIRS * 2 * STATE), (SUBLANES, PAIRS * 2 * STATE))
    eye2 = jnp.eye(2, dtype=F32)
    bb = jnp.stack([bb_re, bb_im]).reshape(2, PAIRS, 2, STATE, GROUP_W)
    t = jnp.einsum('ab,rjaph->jahrbp', eye2, bb).reshape(PAIRS, 2 * GROUP_W, PAIR_W)
    onehot = jax.nn.one_hot(jnp.arange(PAIRS) % 4, 4, dtype=F32)
    bc = jnp.einsum('jq,jrc->jqrc', onehot, t).reshape(PAIRS, LANES, PAIR_W)
    cc = jnp.stack([c_re, -c_im]).reshape(2, C_TILES, 8, 2, GROUP_W, STATE)
    eye8 = jnp.eye(8, dtype=F32)
    cm = jnp.einsum('jk,ab,rnjahp->njrapkbh', eye8, eye2, cc).reshape(C_TILES, C_K, 2 * LANES)
    return a8(a_re), a8(a_im), bc.astype(BF16), cm.astype(BF16)


def _state_to_lanes(re, im):
    b = re.shape[0]
    s = jnp.stack([re.reshape(b, PAIRS, 2, STATE), im.reshape(b, PAIRS, 2, STATE)], axis=2)
    return s.reshape(b, S_W)


def _lanes_to_state(h):
    b = h.shape[0]
    s = h.reshape(b, PAIRS, 2, 2, STATE)
    return s[:, :, 0].reshape(b, GROUPS, STATE), s[:, :, 1].reshape(b, GROUPS, STATE)


ADA_TILE = 1536


def _ada_kernel(c_ref, w_ref, b_ref, o_ref):
    c = c_ref[...]
    o_ref[...] = _bdot(c * _sigmoid(c), w_ref[...]) + b_ref[...]


def _ada(c_all, w_ada, b_ada):
    rows = c_all.shape[0]
    return pl.pallas_call(
        _ada_kernel,
        grid=(DEPTH, 6 * D // ADA_TILE),
        in_specs=[
            pl.BlockSpec((rows, D), lambda l, n: (0, 0)),
            pl.BlockSpec((None, D, ADA_TILE), lambda l, n: (l, 0, n)),
            pl.BlockSpec((None, 1, ADA_TILE), lambda l, n: (l, 0, n)),
        ],
        out_specs=pl.BlockSpec((None, rows, ADA_TILE), lambda l, n: (l, 0, n)),
        out_shape=jax.ShapeDtypeStruct((DEPTH, rows, 6 * D), F32),
        name="adaln",
    )(c_all, w_ada, b_ada.reshape(DEPTH, 1, 6 * D))


def _s5_input_map(x_slab, k, bc_ref, dst_ref, row0, rows):
    xb = x_slab.astype(BF16)
    for q in range(4):
        j = 4 * k + q
        dst_ref[row0:row0 + rows, j * PAIR_W:(j + 1) * PAIR_W] = jnp.dot(
            xb, bc_ref[j], preferred_element_type=F32)


def _s5_output_map(src_ref, row0, rows, cc_ref, n):
    hs = src_ref[row0:row0 + rows, n * C_K:(n + 1) * C_K]
    return jnp.dot(hs.astype(BF16), cc_ref[n], preferred_element_type=F32)


def _mix_back(xb, h, ps, y, mixed_fn, gt, win_ref, gv_ref, dsk_ref, wglu_ref, bglu_ref, wout_ref):
    v = _rms(_gelu(_bdot(h, win_ref[:, D:2 * D])), gv_ref[...])
    y_a = _gelu(_bdot(h, win_ref[:, 0:D])) * mixed_fn(v)
    merged = _sigmoid(_bdot(h, win_ref[:, 3 * D:4 * D])) * y_a
    z = _gelu(y + dsk_ref[...] * ps)
    y_b = z * _sigmoid(_bdot(z, wglu_ref[...]) + bglu_ref[...])
    merged = merged + _sigmoid(_bdot(h, win_ref[:, 4 * D:5 * D])) * y_b
    return xb + gt * _bdot(merged, wout_ref[...]), v


def _mix_prompt_kernel(x_ref, sh_ref, sc_ref, gt_ref, g1_ref, win_ref, gv_ref, wsp_ref, bsp_ref,
                       a8r_ref, a8i_ref, bc_ref, cc_ref, dsk_ref, wglu_ref, bglu_ref, wout_ref,
                       h0_ref, xo_ref, hn_ref, ps_scr, y_scr, hs_scr, wt_scr):
    nb = x_ref.shape[0]
    assert nb == SUBLANES
    seq_per_chunk = ROW_CHUNK // SEQ_TILE

    @pl.when(pl.program_id(0) == 0)
    def _():
        hs_scr[0:SUBLANES, :] = h0_ref[...]

    row = lax.broadcasted_iota(jnp.int32, (CHUNK, CHUNK), 0)
    col = lax.broadcasted_iota(jnp.int32, (CHUNK, CHUNK), 1)
    for hd in range(HEADS):
        wt_scr[hd] = jnp.where(row >= col, wsp_ref[hd], 0.0).astype(BF16)

    def normed(c):
        b0 = c * seq_per_chunk
        xb = x_ref[pl.ds(b0, seq_per_chunk)]
        h = _rms(xb, g1_ref[...]) * (1.0 + sc_ref[pl.ds(b0, seq_per_chunk)]) + sh_ref[pl.ds(b0, seq_per_chunk)]
        return xb, h.reshape(ROW_CHUNK, D).astype(BF16)

    def slab_rows(c, bb):
        return pl.ds(pl.multiple_of((c * seq_per_chunk + bb) * PITCH, SUBLANES), SEQ_TILE)

    def phase_a(c, carry):
        _, h = normed(c)
        ps = _bdot(h, win_ref[:, 2 * D:3 * D])
        for bb in range(seq_per_chunk):
            for k in range(SLABS):
                ps_scr[k, slab_rows(c, bb), :] = ps[bb * SEQ_TILE:(bb + 1) * SEQ_TILE, k * LANES:(k + 1) * LANES]
        return carry

    lax.fori_loop(0, nb // seq_per_chunk, phase_a, 0)

    def phase_b(sub, carry):
        t0 = sub * SUB_T
        for k in range(SLABS):
            xk = jnp.concatenate(
                [ps_scr[k, pl.ds(t0 + t, SUBLANES, stride=PITCH), :] for t in range(SUB_T)], axis=0)
            _s5_input_map(xk, k, bc_ref, hs_scr, SUBLANES, SUB_T * SUBLANES)
        for j in range(PAIRS):
            re = slice(j * PAIR_W, j * PAIR_W + LANES)
            im = slice(j * PAIR_W + LANES, (j + 1) * PAIR_W)
            ar = a8r_ref[:, j * LANES:(j + 1) * LANES]
            ai = a8i_ref[:, j * LANES:(j + 1) * LANES]
            hr = hs_scr[0:SUBLANES, re]
            hi = hs_scr[0:SUBLANES, im]
            for t in range(SUB_T):
                r = slice(SUBLANES * (t + 1), SUBLANES * (t + 2))
                hr, hi = (ar * hr - ai * hi + hs_scr[r, re], ar * hi + ai * hr + hs_scr[r, im])
                hs_scr[r, re] = hr
                hs_scr[r, im] = hi
            hs_scr[0:SUBLANES, re] = hr
            hs_scr[0:SUBLANES, im] = hi
        for n in range(C_TILES):
            yn = _s5_output_map(hs_scr, SUBLANES, SUB_T * SUBLANES, cc_ref, n)
            for kk in range(2):
                for t in range(SUB_T):
                    y_scr[2 * n + kk, pl.ds(t0 + t, SUBLANES, stride=PITCH), :] = (
                        yn[t * SUBLANES:(t + 1) * SUBLANES, kk * LANES:(kk + 1) * LANES])
        return carry

    lax.fori_loop(0, SEQ_TILE // SUB_T, phase_b, 0)
    hn_ref[...] = hs_scr[0:SUBLANES, :]

    def mixed_fn(v):
        vb = v.astype(BF16)
        rows = []
        for bb in range(seq_per_chunk):
            heads = [jnp.dot(wt_scr[hd], vb[bb * CHUNK:(bb + 1) * CHUNK, hd * HEAD_DIM:(hd + 1) * HEAD_DIM],
                             preferred_element_type=F32) for hd in range(HEADS)]
            rows.append(jnp.concatenate(heads, axis=1) + bsp_ref[...])
        return jnp.concatenate(rows, axis=0)

    def phase_c(c, carry):
        b0 = c * seq_per_chunk
        xb, h = normed(c)
        gather = lambda scr: jnp.concatenate(
            [jnp.concatenate([scr[k, slab_rows(c, bb), :] for k in range(SLABS)], axis=1)
             for bb in range(seq_per_chunk)], axis=0)
        gt = jnp.broadcast_to(gt_ref[pl.ds(b0, seq_per_chunk)], (seq_per_chunk, SEQ_TILE, D)).reshape(ROW_CHUNK, D)
        xn, _ = _mix_back(xb.reshape(ROW_CHUNK, D), h, gather(ps_scr), gather(y_scr), mixed_fn, gt,
                          win_ref, gv_ref, dsk_ref, wglu_ref, bglu_ref, wout_ref)
        xo_ref[pl.ds(b0, seq_per_chunk)] = xn.reshape(seq_per_chunk, SEQ_TILE, D)
        return carry

    lax.fori_loop(0, nb // seq_per_chunk, phase_c, 0)


def _mix_prompt(x, sh, sc, gt, lw):
    nb, seq, _ = x.shape
    assert nb == SUBLANES and seq % SEQ_TILE == 0
    tile = pl.BlockSpec((nb, SEQ_TILE, D), lambda s: (0, s, 0))
    consts = [sh, sc, gt, lw['g1'], lw['w_in'], lw['g_v'], lw['w_sp'], lw['b_sp'], lw['a8r'], lw['a8i'],
              lw['bc'], lw['cc'], lw['d_skip'], lw['w_glu'], lw['b_glu'], lw['w_out'],
              jnp.zeros((nb, S_W), F32)]
    return pl.pallas_call(
        _mix_prompt_kernel,
        grid=(seq // SEQ_TILE,),
        in_specs=[tile] + [_const_spec(a.shape) for a in consts],
        out_specs=(tile, pl.BlockSpec((nb, S_W), lambda s: (0, 0))),
        out_shape=(jax.ShapeDtypeStruct(x.shape, F32), jax.ShapeDtypeStruct((nb, S_W), F32)),
        scratch_shapes=[
            pltpu.VMEM((SLABS, nb * PITCH, LANES), F32),
            pltpu.VMEM((SLABS, nb * PITCH, LANES), F32),
            pltpu.VMEM((SUBLANES * (SUB_T + 1), S_W), F32),
            pltpu.VMEM((HEADS, CHUNK, CHUNK), BF16),
        ],
        compiler_params=pltpu.CompilerParams(
            dimension_semantics=("arbitrary",), vmem_limit_bytes=VMEM_LIMIT),
        name="mix_prompt",
    )(x, *consts)


def _mix_sample_kernel(x_ref, sh_ref, sc_ref, gt_ref, g1_ref, win_ref, gv_ref, wd_ref, bd_ref,
                       a8r_ref, a8i_ref, bc_ref, cc_ref, dsk_ref, wglu_ref, bglu_ref, wout_ref,
                       h0_ref, xo_ref, hn_ref, v_ref):
    rows = x_ref.shape[0]
    xb = x_ref[...]
    h = (_rms(xb, g1_ref[...]) * (1.0 + sc_ref[...]) + sh_ref[...]).astype(BF16)
    ps = _bdot(h, win_ref[:, 2 * D:3 * D])
    for k in range(SLABS):
        _s5_input_map(ps[:, k * LANES:(k + 1) * LANES], k, bc_ref, hn_ref, 0, rows)
    for j in range(PAIRS):
        re = slice(j * PAIR_W, j * PAIR_W + LANES)
        im = slice(j * PAIR_W + LANES, (j + 1) * PAIR_W)
        ar = a8r_ref[0:1, j * LANES:(j + 1) * LANES]
        ai = a8i_ref[0:1, j * LANES:(j + 1) * LANES]
        hr, hi = h0_ref[:, re], h0_ref[:, im]
        hn_ref[:, re] = ar * hr - ai * hi + hn_ref[:, re]
        hn_ref[:, im] = ar * hi + ai * hr + hn_ref[:, im]
    y = jnp.concatenate([_s5_output_map(hn_ref, 0, rows, cc_ref, n) for n in range(C_TILES)], axis=1)
    mixed_fn = lambda v: wd_ref[...] * v + bd_ref[...]
    xn, v = _mix_back(xb, h, ps, y, mixed_fn, gt_ref[...], win_ref, gv_ref, dsk_ref, wglu_ref, bglu_ref, wout_ref)
    xo_ref[...] = xn
    v_ref[...] = v


def _mix_sample(x, sh, sc, gt, h0, lw):
    rows = x.shape[0]
    args = [x, sh, sc, gt, lw['g1'], lw['w_in'], lw['g_v'], lw['w_diag'], lw['b_diag'], lw['a8r'], lw['a8i'],
            lw['bc'], lw['cc'], lw['d_skip'], lw['w_glu'], lw['b_glu'], lw['w_out'], h0]
    return pl.pallas_call(
        _mix_sample_kernel,
        out_shape=(jax.ShapeDtypeStruct((rows, D), F32), jax.ShapeDtypeStruct((rows, S_W), F32),
                   jax.ShapeDtypeStruct((rows, D), F32)),
        compiler_params=pltpu.CompilerParams(vmem_limit_bytes=VMEM_LIMIT),
        name="mix_sample",
    )(*args)


def _mlp_kernel(x_ref, sh_ref, sc_ref, gt_ref, g2_ref, w1_ref, w2_ref, gf_ref, xo_ref, *, final):
    x = x_ref[...]
    h = (_rms(x, g2_ref[...]) * (1.0 + sc_ref[...]) + sh_ref[...]).astype(BF16)
    acc = jnp.zeros(x.shape, F32)
    for k in range(D_FF // D):
        a = jnp.dot(h, w1_ref[:, k * D:(k + 1) * D], preferred_element_type=F32)
        acc = acc + _bdot(jnp.square(jnp.maximum(a, 0.0)), w2_ref[k * D:(k + 1) * D, :])
    xn = x + gt_ref[...] * acc
    if final:
        xn = _rms(xn, gf_ref[...])
    xo_ref[...] = xn


def _mlp(x2d, sh, sc, gt, lw, g_final, final, rows_per_mod):
    rows = x2d.shape[0]
    tm = min(MLP_ROWS, rows_per_mod) if rows_per_mod > 1 else rows
    if rows_per_mod > 1:
        assert rows_per_mod % tm == 0
        per = rows_per_mod // tm
        mod_spec = pl.BlockSpec((None, 1, D), lambda i: (i // per, 0, 0))
    else:
        sh, sc, gt = (a.reshape(rows, D) for a in (sh, sc, gt))
        mod_spec = pl.BlockSpec((tm, D), lambda i: (i, 0))
    tile = pl.BlockSpec((tm, D), lambda i: (i, 0))
    return pl.pallas_call(
        functools.partial(_mlp_kernel, final=final),
        grid=(rows // tm,),
        in_specs=[tile, mod_spec, mod_spec, mod_spec, _const_spec((1, D)),
                  _const_spec((D, D_FF)), _const_spec((D_FF, D)), _const_spec((1, D))],
        out_specs=tile,
        out_shape=jax.ShapeDtypeStruct((rows, D), F32),
        compiler_params=pltpu.CompilerParams(
            dimension_semantics=("arbitrary",), vmem_limit_bytes=VMEM_LIMIT),
        name="mlp_final" if final else "mlp",
    )(x2d, sh, sc, gt, lw['g2'], lw['w_ff1'], lw['w_ff2'], g_final)


def kernel(x_prompt, x_sample, c_prompt, c_sample, state_ssm_re, state_ssm_im, w_ada, b_ada, g_norm1, g_norm2, w_in, g_v, w_spatial, b_spatial, lam_re, lam_im, log_dt, b_re, b_im, c_re, c_im, d_skip, w_glu, b_glu, w_out, w_ff1, w_ff2, g_final):
    nbp, seq, _ = x_prompt.shape
    nbs = x_sample.shape[0]
    assert x_sample.shape[1] == 1

    a_re, a_im, bb_re, bb_im = _s5_prep(lam_re, lam_im, log_dt, b_re, b_im)
    mod = _ada(jnp.concatenate([c_prompt, c_sample], axis=0), w_ada.astype(BF16), b_ada)

    xp = x_prompt
    xs = x_sample.reshape(nbs, D)
    re_p, im_p, re_s, im_s, v_s = [], [], [], [], []
    for l in range(DEPTH):
        a8r, a8i, bc, cc = _s5_matrices(a_re[l], a_im[l], bb_re[l], bb_im[l], c_re[l], c_im[l])
        row = lambda a: a[l].reshape(1, D)
        lw = dict(
            g1=row(g_norm1), g2=row(g_norm2), g_v=row(g_v), d_skip=row(d_skip), b_glu=row(b_glu),
            w_in=w_in[l].astype(BF16), w_glu=w_glu[l].astype(BF16), w_out=w_out[l].astype(BF16),
            w_ff1=w_ff1[l].astype(BF16), w_ff2=w_ff2[l].astype(BF16),
            w_sp=w_spatial[l], b_sp=jnp.repeat(b_spatial[l].T, HEAD_DIM, axis=1),
            w_diag=jnp.repeat(w_spatial[l][:, 0, 0], HEAD_DIM).reshape(1, D),
            b_diag=jnp.repeat(b_spatial[l][:, 0], HEAD_DIM).reshape(1, D),
            a8r=a8r, a8i=a8i, bc=bc, cc=cc)
        gf = g_final.reshape(1, D)
        final = l == DEPTH - 1

        sh1, sc1, gt1, sh2, sc2, gt2 = (m.reshape(nbp, 1, D) for m in jnp.split(mod[l, :nbp], 6, axis=-1))
        xp, hn = _mix_prompt(xp, sh1, sc1, gt1, lw)
        xp = _mlp(xp.reshape(nbp * seq, D), sh2, sc2, gt2, lw, gf, final, seq).reshape(nbp, seq, D)
        r, i = _lanes_to_state(hn)
        re_p.append(r)
        im_p.append(i)

        sh1, sc1, gt1, sh2, sc2, gt2 = jnp.split(mod[l, nbp:], 6, axis=-1)
        xs, hn, v = _mix_sample(xs, sh1, sc1, gt1, _state_to_lanes(state_ssm_re[l], state_ssm_im[l]), lw)
        xs = _mlp(xs, sh2, sc2, gt2, lw, gf, final, 1)
        r, i = _lanes_to_state(hn)
        re_s.append(r)
        im_s.append(i)
        v_s.append(v.reshape(nbs, 1, D))

    return (xp, xs.reshape(nbs, 1, D), jnp.stack(re_p), jnp.stack(im_p),
            jnp.stack(re_s), jnp.stack(im_s), jnp.stack(v_s))
```

```python
import functools
import math

import jax
import jax.numpy as jnp
from jax import lax
from jax.experimental import pallas as pl
from jax.experimental.pallas import tpu as pltpu

F32 = jnp.float32
BF16 = jnp.bfloat16

D = 1024
DEPTH = 2
CHUNK = 128
HEADS = 4
HEAD_DIM = D // HEADS
GROUPS = 64
GROUP_W = 16
STATE = 64
D_FF = 4 * D
N_IN = 5 * D
EPS = 1e-6

LANES = 128
SUBLANES = 8
PAIRS = GROUPS // 2
PAIR_W = 4 * STATE
S_W = PAIRS * PAIR_W
S_HALF = S_W // 2
SLABS = D // LANES
C_TILES = 4
C_K = S_W // C_TILES

SEQ_TILE = CHUNK
ROW_CHUNK = 256
SEQ_PER_CHUNK = ROW_CHUNK // SEQ_TILE
SUB_T = 16
PITCH = SEQ_TILE + SUBLANES
MLP_ROWS = 512
VMEM_LIMIT = 60 * 1024 * 1024

_GELU_C = math.sqrt(2.0 / math.pi)


def _gelu(x):
    return x * (0.5 * (1.0 + jnp.tanh(_GELU_C * (x + 0.044715 * (x * x * x)))))


def _sigmoid(x):
    return 1.0 / (1.0 + jnp.exp(-x))


def _rms(x, g):
    return x * lax.rsqrt(jnp.mean(x * x, axis=-1, keepdims=True) + EPS) * g


def _bdot(a, b):
    return jnp.dot(a.astype(BF16), b, preferred_element_type=F32)


def _layer_spec(a, l):
    zeros = (0,) * (a.ndim - 1)
    return pl.BlockSpec((None,) + a.shape[1:], lambda *_: (l,) + zeros, pipeline_mode=pl.Buffered(1))


def _s5_prep_kernel(lr_ref, li_ref, ldt_ref, lrw_ref, liw_ref, ldtw_ref, br_ref, bi_ref,
                    ar_ref, ai_ref, bbr_ref, bbi_ref):
    def a_bar(lr, li, ldt):
        dt = jnp.exp(ldt)
        mag = jnp.exp(lr * dt)
        return mag * jnp.cos(li * dt), mag * jnp.sin(li * dt)

    ar, ai = a_bar(lr_ref[...], li_ref[...], ldt_ref[...])
    ar_ref[...] = ar
    ai_ref[...] = ai
    lr, li = lrw_ref[...], liw_ref[...]
    ar, ai = a_bar(lr, li, ldtw_ref[...])
    nr, ni = ar - 1.0, ai
    den = lr * lr + li * li
    cr = (nr * lr + ni * li) / den
    ci = (ni * lr - nr * li) / den
    br, bi = br_ref[...], bi_ref[...]
    bbr_ref[...] = cr * br - ci * bi
    bbi_ref[...] = cr * bi + ci * br


def _s5_prep(lam_re, lam_im, log_dt, b_re, b_im):
    wide = (DEPTH, GROUPS, STATE * GROUP_W)
    ldt = jnp.broadcast_to(log_dt[:, :, None], lam_re.shape)
    lam_w = jnp.repeat(jnp.stack([lam_re, lam_im, ldt]), GROUP_W, axis=-1)
    small = jax.ShapeDtypeStruct(lam_re.shape, F32)
    big = jax.ShapeDtypeStruct(wide, F32)
    return pl.pallas_call(
        _s5_prep_kernel, out_shape=(small, small, big, big), name="s5_prep",
    )(lam_re, lam_im, ldt, lam_w[0], lam_w[1], lam_w[2], b_re.reshape(wide), b_im.reshape(wide))


def _s5_matrices(bb_re, bb_im, c_re, c_im):
    eye2 = jnp.eye(2, dtype=F32)
    bb = jnp.stack([bb_re, bb_im]).reshape(2, DEPTH, PAIRS, 2, STATE, GROUP_W)
    t = jnp.einsum('ab,rljaph->ljahrbp', eye2, bb).reshape(DEPTH, PAIRS, 2 * GROUP_W, PAIR_W)
    onehot = jax.nn.one_hot(jnp.arange(PAIRS) % 4, 4, dtype=F32)
    bc = jnp.einsum('jq,ljrc->ljqrc', onehot, t).reshape(DEPTH, PAIRS, LANES, PAIR_W)
    cc = jnp.stack([c_re, -c_im]).reshape(2, DEPTH, C_TILES, 8, 2, GROUP_W, STATE)
    eye8 = jnp.eye(8, dtype=F32)
    cm = jnp.einsum('jk,ab,rlnjahp->lnjrapkbh', eye8, eye2, cc).reshape(DEPTH, C_TILES, C_K, 2 * LANES)
    return bc.astype(BF16), cm.astype(BF16)


ADA_TILE = 1536


def _ada_kernel(c_ref, w_ref, b_ref, o_ref):
    c = c_ref[...]
    o_ref[...] = _bdot(c * _sigmoid(c), w_ref[...]) + b_ref[...]


def _ada(c_all, w_ada, b_ada):
    rows = c_all.shape[0]
    return pl.pallas_call(
        _ada_kernel,
        grid=(DEPTH, 6 * D // ADA_TILE),
        in_specs=[
            pl.BlockSpec((rows, D), lambda l, n: (0, 0)),
            pl.BlockSpec((None, D, ADA_TILE), lambda l, n: (l, 0, n)),
            pl.BlockSpec((None, 1, ADA_TILE), lambda l, n: (l, 0, n)),
        ],
        out_specs=pl.BlockSpec((None, rows, ADA_TILE), lambda l, n: (l, 0, n)),
        out_shape=jax.ShapeDtypeStruct((DEPTH, rows, 6 * D), F32),
        name="adaln",
    )(c_all, w_ada, b_ada.reshape(DEPTH, 1, 6 * D))


def _mod_specs(l, rows, row_block, which):
    return [pl.BlockSpec((None, rows, D), lambda *_, n=n: (l, row_block, n)) for n in which]


def _s5_input_map(x_slab, k, bc_ref, dst_ref, row0, rows):
    xb = x_slab.astype(BF16)
    for q in range(4):
        j = 4 * k + q
        dst_ref[row0:row0 + rows, j * PAIR_W:(j + 1) * PAIR_W] = jnp.dot(
            xb, bc_ref[j], preferred_element_type=F32)


def _s5_output_map(src_ref, row0, rows, cc_ref, n):
    hs = src_ref[row0:row0 + rows, n * C_K:(n + 1) * C_K]
    return jnp.dot(hs.astype(BF16), cc_ref[n], preferred_element_type=F32)


def _mix_back(xb, h, ps, y, mixed_fn, gt, win_ref, gv_ref, dsk_ref, wglu_ref, bglu_ref, wout_ref):
    v = _rms(_gelu(_bdot(h, win_ref[:, D:2 * D])), gv_ref[...])
    y_a = _gelu(_bdot(h, win_ref[:, 0:D])) * mixed_fn(v)
    merged = _sigmoid(_bdot(h, win_ref[:, 3 * D:4 * D])) * y_a
    z = _gelu(y + dsk_ref[...] * ps)
    y_b = z * _sigmoid(_bdot(z, wglu_ref[...]) + bglu_ref[...])
    merged = merged + _sigmoid(_bdot(h, win_ref[:, 4 * D:5 * D])) * y_b
    return xb + gt * _bdot(merged, wout_ref[...]), v


def _mix_prompt_kernel(x_ref, sh_ref, sc_ref, gt_ref, g1_ref, win_ref, gv_ref, wsp_ref, bsp_ref,
                       ar_ref, ai_ref, bc_ref, cc_ref, dsk_ref, wglu_ref, bglu_ref, wout_ref,
                       xo_ref, hre_ref, him_ref, ps_scr, y_scr, hs_scr, wt_scr):
    nb = x_ref.shape[0]
    assert nb == SUBLANES
    step = pl.program_id(0)

    @pl.when(step == 0)
    def _():
        hs_scr[0:SUBLANES, :] = jnp.zeros((SUBLANES, S_W), F32)

    row = lax.broadcasted_iota(jnp.int32, (CHUNK, CHUNK), 0)
    col = lax.broadcasted_iota(jnp.int32, (CHUNK, CHUNK), 1)
    for hd in range(HEADS):
        wt_scr[hd] = jnp.where(row >= col, wsp_ref[hd], 0.0).astype(BF16)

    def load_rows(c):
        xs, hs = [], []
        for bb in range(SEQ_PER_CHUNK):
            b = c * SEQ_PER_CHUNK + bb
            x = x_ref[b]
            xs.append(x)
            hs.append(_rms(x, g1_ref[...]) * (1.0 + sc_ref[pl.ds(b, 1), :]) + sh_ref[pl.ds(b, 1), :])
        return jnp.concatenate(xs, axis=0), jnp.concatenate(hs, axis=0).astype(BF16)

    def slab_rows(c, bb):
        return pl.ds(pl.multiple_of((c * SEQ_PER_CHUNK + bb) * PITCH, SUBLANES), SEQ_TILE)

    def phase_a(c, carry):
        _, h = load_rows(c)
        ps = _bdot(h, win_ref[:, 2 * D:3 * D])
        for bb in range(SEQ_PER_CHUNK):
            for k in range(SLABS):
                ps_scr[k, slab_rows(c, bb), :] = ps[bb * SEQ_TILE:(bb + 1) * SEQ_TILE, k * LANES:(k + 1) * LANES]
        return carry

    lax.fori_loop(0, nb // SEQ_PER_CHUNK, phase_a, 0)

    def phase_b(sub, carry):
        t0 = sub * SUB_T
        for k in range(SLABS):
            xk = jnp.concatenate(
                [ps_scr[k, pl.ds(t0 + t, SUBLANES, stride=PITCH), :] for t in range(SUB_T)], axis=0)
            _s5_input_map(xk, k, bc_ref, hs_scr, SUBLANES, SUB_T * SUBLANES)
        for j in range(PAIRS):
            re = slice(j * PAIR_W, j * PAIR_W + LANES)
            im = slice(j * PAIR_W + LANES, (j + 1) * PAIR_W)
            ar = jnp.broadcast_to(ar_ref[:, j * LANES:(j + 1) * LANES], (SUBLANES, LANES))
            ai = jnp.broadcast_to(ai_ref[:, j * LANES:(j + 1) * LANES], (SUBLANES, LANES))
            hr = hs_scr[0:SUBLANES, re]
            hi = hs_scr[0:SUBLANES, im]
            for t in range(SUB_T):
                r = slice(SUBLANES * (t + 1), SUBLANES * (t + 2))
                hr, hi = (ar * hr - ai * hi + hs_scr[r, re], ar * hi + ai * hr + hs_scr[r, im])
                hs_scr[r, re] = hr
                hs_scr[r, im] = hi
            hs_scr[0:SUBLANES, re] = hr
            hs_scr[0:SUBLANES, im] = hi
        for n in range(C_TILES):
            yn = _s5_output_map(hs_scr, SUBLANES, SUB_T * SUBLANES, cc_ref, n)
            for kk in range(2):
                for t in range(SUB_T):
                    y_scr[2 * n + kk, pl.ds(t0 + t, SUBLANES, stride=PITCH), :] = (
                        yn[t * SUBLANES:(t + 1) * SUBLANES, kk * LANES:(kk + 1) * LANES])
        return carry

    lax.fori_loop(0, SEQ_TILE // SUB_T, phase_b, 0)

    @pl.when(step == pl.num_programs(0) - 1)
    def _():
        for j in range(PAIRS):
            hre_ref[:, j * LANES:(j + 1) * LANES] = hs_scr[0:SUBLANES, j * PAIR_W:j * PAIR_W + LANES]
            him_ref[:, j * LANES:(j + 1) * LANES] = hs_scr[0:SUBLANES, j * PAIR_W + LANES:(j + 1) * PAIR_W]

    def mixed_fn(v):
        vb = v.astype(BF16)
        rows = []
        for bb in range(SEQ_PER_CHUNK):
            heads = [jnp.dot(wt_scr[hd], vb[bb * CHUNK:(bb + 1) * CHUNK, hd * HEAD_DIM:(hd + 1) * HEAD_DIM],
                             preferred_element_type=F32) for hd in range(HEADS)]
            rows.append(jnp.concatenate(heads, axis=1) + bsp_ref[...])
        return jnp.concatenate(rows, axis=0)

    def phase_c(c, carry):
        xb, h = load_rows(c)
        gather = lambda scr: jnp.concatenate(
            [jnp.concatenate([scr[k, slab_rows(c, bb), :] for k in range(SLABS)], axis=1)
             for bb in range(SEQ_PER_CHUNK)], axis=0)
        gt = jnp.concatenate(
            [jnp.broadcast_to(gt_ref[pl.ds(c * SEQ_PER_CHUNK + bb, 1), :], (SEQ_TILE, D))
             for bb in range(SEQ_PER_CHUNK)], axis=0)
        xn, _ = _mix_back(xb, h, gather(ps_scr), gather(y_scr), mixed_fn, gt,
                          win_ref, gv_ref, dsk_ref, wglu_ref, bglu_ref, wout_ref)
        for bb in range(SEQ_PER_CHUNK):
            xo_ref[c * SEQ_PER_CHUNK + bb] = xn[bb * SEQ_TILE:(bb + 1) * SEQ_TILE]
        return carry

    lax.fori_loop(0, nb // SEQ_PER_CHUNK, phase_c, 0)


def _mix_prompt(x, mod, mod_row_block, l, p):
    nb, seq, _ = x.shape
    assert nb == SUBLANES and seq % SEQ_TILE == 0
    tile = pl.BlockSpec((nb, SEQ_TILE, D), lambda s: (0, s, 0))
    consts = [p['g1'], p['w_in'], p['g_v'], p['w_sp'], p['b_sp'], p['a_re'], p['a_im'],
              p['bc'], p['cc'], p['d_skip'], p['w_glu'], p['b_glu'], p['w_out']]
    state = jax.ShapeDtypeStruct((nb, S_HALF), F32)
    state_spec = pl.BlockSpec((nb, S_HALF), lambda s: (0, 0))
    return pl.pallas_call(
        _mix_prompt_kernel,
        grid=(seq // SEQ_TILE,),
        in_specs=[tile] + _mod_specs(l, nb, mod_row_block, (0, 1, 2)) + [_layer_spec(a, l) for a in consts],
        out_specs=(tile, state_spec, state_spec),
        out_shape=(jax.ShapeDtypeStruct(x.shape, F32), state, state),
        scratch_shapes=[
            pltpu.VMEM((SLABS, nb * PITCH, LANES), F32),
            pltpu.VMEM((SLABS, nb * PITCH, LANES), F32),
            pltpu.VMEM((SUBLANES * (SUB_T + 1), S_W), F32),
            pltpu.VMEM((HEADS, CHUNK, CHUNK), BF16),
        ],
        compiler_params=pltpu.CompilerParams(
            dimension_semantics=("arbitrary",), vmem_limit_bytes=VMEM_LIMIT),
        name="mix_prompt",
    )(x, mod, mod, mod, *consts)


def _mix_sample_kernel(x_ref, sh_ref, sc_ref, gt_ref, g1_ref, win_ref, gv_ref, wd_ref, bd_ref,
                       ar_ref, ai_ref, bc_ref, cc_ref, dsk_ref, wglu_ref, bglu_ref, wout_ref,
                       h0r_ref, h0i_ref, xo_ref, hre_ref, him_ref, v_ref, hs_scr):
    rows = x_ref.shape[0]
    xb = x_ref[...]
    h = (_rms(xb, g1_ref[...]) * (1.0 + sc_ref[...]) + sh_ref[...]).astype(BF16)
    ps = _bdot(h, win_ref[:, 2 * D:3 * D])
    for k in range(SLABS):
        _s5_input_map(ps[:, k * LANES:(k + 1) * LANES], k, bc_ref, hs_scr, 0, rows)
    for j in range(PAIRS):
        half = slice(j * LANES, (j + 1) * LANES)
        re = slice(j * PAIR_W, j * PAIR_W + LANES)
        im = slice(j * PAIR_W + LANES, (j + 1) * PAIR_W)
        ar, ai = ar_ref[:, half], ai_ref[:, half]
        hr, hi = h0r_ref[:, half], h0i_ref[:, half]
        nr = ar * hr - ai * hi + hs_scr[:, re]
        ni = ar * hi + ai * hr + hs_scr[:, im]
        hs_scr[:, re] = nr
        hs_scr[:, im] = ni
        hre_ref[:, half] = nr
        him_ref[:, half] = ni
    y = jnp.concatenate([_s5_output_map(hs_scr, 0, rows, cc_ref, n) for n in range(C_TILES)], axis=1)
    mixed_fn = lambda v: wd_ref[...] * v + bd_ref[...]
    xn, v = _mix_back(xb, h, ps, y, mixed_fn, gt_ref[...], win_ref, gv_ref, dsk_ref, wglu_ref, bglu_ref, wout_ref)
    xo_ref[...] = xn
    v_ref[...] = v


def _mix_sample(x, mod, l, h0_re, h0_im, p):
    rows = x.shape[0]
    consts = [p['g1'], p['w_in'], p['g_v'], p['w_diag'], p['b_diag'], p['a_re'], p['a_im'],
              p['bc'], p['cc'], p['d_skip'], p['w_glu'], p['b_glu'], p['w_out'], h0_re, h0_im]
    whole = lambda shape: pl.BlockSpec(shape, lambda i: (0,) * len(shape))
    state = jax.ShapeDtypeStruct((rows, S_HALF), F32)
    return pl.pallas_call(
        _mix_sample_kernel,
        grid=(1,),
        in_specs=[whole((rows, D))] + _mod_specs(l, rows, 0, (0, 1, 2)) + [_layer_spec(a, l) for a in consts],
        out_specs=(whole((rows, D)), whole((rows, S_HALF)), whole((rows, S_HALF)), whole((rows, D))),
        out_shape=(jax.ShapeDtypeStruct((rows, D), F32), state, state, jax.ShapeDtypeStruct((rows, D), F32)),
        scratch_shapes=[pltpu.VMEM((rows, S_W), F32)],
        compiler_params=pltpu.CompilerParams(
            dimension_semantics=("arbitrary",), vmem_limit_bytes=VMEM_LIMIT),
        name="mix_sample",
    )(x, mod, mod, mod, *consts)


def _mlp_kernel(x_ref, sh_ref, sc_ref, gt_ref, g2_ref, w1_ref, w2_ref, gf_ref, xo_ref, *, final, tiles_per_mod):
    x = x_ref[...]
    if tiles_per_mod:
        b = pl.program_id(0) // tiles_per_mod
        sh, sc, gt = (r[pl.ds(b, 1), :] for r in (sh_ref, sc_ref, gt_ref))
    else:
        sh, sc, gt = sh_ref[...], sc_ref[...], gt_ref[...]
    h = (_rms(x, g2_ref[...]) * (1.0 + sc) + sh).astype(BF16)
    acc = jnp.zeros(x.shape, F32)
    for k in range(D_FF // D):
        a = jnp.dot(h, w1_ref[:, k * D:(k + 1) * D], preferred_element_type=F32)
        acc = acc + _bdot(jnp.square(jnp.maximum(a, 0.0)), w2_ref[k * D:(k + 1) * D, :])
    xn = x + gt * acc
    if final:
        xn = _rms(xn, gf_ref[...])
    xo_ref[...] = xn


def _mlp(x, mod, mod_row_block, l, p, g_final, final):
    if x.ndim == 3:
        nb, seq, _ = x.shape
        tm = min(MLP_ROWS, seq)
        assert seq % tm == 0
        per = seq // tm
        grid = (nb * per,)
        tile = pl.BlockSpec((None, tm, D), lambda i: (i // per, i % per, 0))
        mod_rows = nb
    else:
        per = 0
        grid = (1,)
        tile = pl.BlockSpec(x.shape, lambda i: (0, 0))
        mod_rows = x.shape[0]
    return pl.pallas_call(
        functools.partial(_mlp_kernel, final=final, tiles_per_mod=per),
        grid=grid,
        in_specs=[tile] + _mod_specs(l, mod_rows, mod_row_block, (3, 4, 5)) + [
            _layer_spec(p['g2'], l), _layer_spec(p['w_ff1'], l), _layer_spec(p['w_ff2'], l),
            pl.BlockSpec((1, D), lambda i: (0, 0))],
        out_specs=tile,
        out_shape=jax.ShapeDtypeStruct(x.shape, F32),
        compiler_params=pltpu.CompilerParams(
            dimension_semantics=("arbitrary",), vmem_limit_bytes=VMEM_LIMIT),
        name="mlp_final" if final else "mlp",
    )(x, mod, mod, mod, p['g2'], p['w_ff1'], p['w_ff2'], g_final)


def kernel(x_prompt, x_sample, c_prompt, c_sample, state_ssm_re, state_ssm_im, w_ada, b_ada, g_norm1, g_norm2, w_in, g_v, w_spatial, b_spatial, lam_re, lam_im, log_dt, b_re, b_im, c_re, c_im, d_skip, w_glu, b_glu, w_out, w_ff1, w_ff2, g_final):
    nbp, seq, _ = x_prompt.shape
    nbs = x_sample.shape[0]
    assert x_sample.shape[1] == 1 and nbs % nbp == 0

    a_re, a_im, bb_re, bb_im = _s5_prep(lam_re, lam_im, log_dt, b_re, b_im)
    bc, cc = _s5_matrices(bb_re, bb_im, c_re, c_im)
    mod = _ada(jnp.concatenate([c_sample, c_prompt], axis=0), w_ada.astype(BF16), b_ada)
    prompt_row_block = nbs // nbp

    vec = lambda a: a.reshape(DEPTH, 1, D)
    p = dict(
        g1=vec(g_norm1), g2=vec(g_norm2), g_v=vec(g_v), d_skip=vec(d_skip), b_glu=vec(b_glu),
        w_in=w_in.astype(BF16), w_glu=w_glu.astype(BF16), w_out=w_out.astype(BF16),
        w_ff1=w_ff1.astype(BF16), w_ff2=w_ff2.astype(BF16),
        w_sp=w_spatial, b_sp=jnp.repeat(jnp.swapaxes(b_spatial, 1, 2), HEAD_DIM, axis=2),
        w_diag=vec(jnp.repeat(w_spatial[:, :, 0, 0], HEAD_DIM, axis=1)),
        b_diag=vec(jnp.repeat(b_spatial[:, :, 0], HEAD_DIM, axis=1)),
        a_re=a_re.reshape(DEPTH, 1, S_HALF), a_im=a_im.reshape(DEPTH, 1, S_HALF), bc=bc, cc=cc)
    gf = g_final.reshape(1, D)
    h0_re = state_ssm_re.reshape(DEPTH, nbs, S_HALF)
    h0_im = state_ssm_im.reshape(DEPTH, nbs, S_HALF)

    xp = x_prompt
    xs = x_sample.reshape(nbs, D)
    re_p, im_p, re_s, im_s, v_s = [], [], [], [], []
    for l in range(DEPTH):
        final = l == DEPTH - 1
        xp, hr, hi = _mix_prompt(xp, mod, prompt_row_block, l, p)
        xp = _mlp(xp, mod, prompt_row_block, l, p, gf, final)
        re_p.append(hr)
        im_p.append(hi)
        xs, hr, hi, v = _mix_sample(xs, mod, l, h0_re, h0_im, p)
        xs = _mlp(xs, mod, 0, l, p, gf, final)
        re_s.append(hr)
        im_s.append(hi)
        v_s.append(v)

    state = lambda hs, nb: jnp.stack(hs).reshape(DEPTH, nb, GROUPS, STATE)
    return (xp, xs.reshape(nbs, 1, D), state(re_p, nbp), state(im_p, nbp),
            state(re_s, nbs), state(im_s, nbs), jnp.stack(v_s).reshape(DEPTH, nbs, 1, D))
```

```python
import functools
import math

import jax
import jax.numpy as jnp
import numpy as np
from jax import lax
from jax.experimental import pallas as pl
from jax.experimental.pallas import tpu as pltpu

F32 = jnp.float32
BF16 = jnp.bfloat16

D = 1024
DEPTH = 2
CHUNK = 128
HEADS = 4
HEAD_DIM = D // HEADS
GROUPS = 64
GROUP_W = 16
STATE = 64
D_FF = 4 * D
N_IN = 5 * D
EPS = 1e-6

LANES = 128
SUBLANES = 8
PAIRS = GROUPS // 2
PAIR_W = 4 * STATE
S_W = PAIRS * PAIR_W
S_HALF = S_W // 2
SLABS = D // LANES
C_TILES = 4
C_K = S_W // C_TILES

SEQ_TILE = CHUNK
ROW_CHUNK = 256
SEQ_PER_CHUNK = ROW_CHUNK // SEQ_TILE
SUB_T = 16
PITCH = SEQ_TILE + SUBLANES
MLP_ROWS = 512
VMEM_LIMIT = 60 * 1024 * 1024

_GELU_C = math.sqrt(2.0 / math.pi)


def _gelu(x):
    return x * (0.5 * (1.0 + jnp.tanh(_GELU_C * (x + 0.044715 * (x * x * x)))))


def _sigmoid(x):
    return 1.0 / (1.0 + jnp.exp(-x))


def _rms(x, g):
    return x * lax.rsqrt(jnp.mean(x * x, axis=-1, keepdims=True) + EPS) * g


def _bdot(a, b):
    return jnp.dot(a.astype(BF16), b, preferred_element_type=F32)


def _layer_spec(a, l):
    zeros = (0,) * (a.ndim - 1)
    return pl.BlockSpec((None,) + a.shape[1:], lambda *_: (l,) + zeros, pipeline_mode=pl.Buffered(1))


def _s5_prep_kernel(lr_ref, li_ref, ldt_ref, lrw_ref, liw_ref, ldtw_ref, br_ref, bi_ref,
                    ar_ref, ai_ref, bbr_ref, bbi_ref):
    def a_bar(lr, li, ldt):
        dt = jnp.exp(ldt)
        mag = jnp.exp(lr * dt)
        return mag * jnp.cos(li * dt), mag * jnp.sin(li * dt)

    ar, ai = a_bar(lr_ref[...], li_ref[...], ldt_ref[...])
    ar_ref[...] = ar
    ai_ref[...] = ai
    lr, li = lrw_ref[...], liw_ref[...]
    ar, ai = a_bar(lr, li, ldtw_ref[...])
    nr, ni = ar - 1.0, ai
    den = lr * lr + li * li
    cr = (nr * lr + ni * li) / den
    ci = (ni * lr - nr * li) / den
    br, bi = br_ref[...], bi_ref[...]
    bbr_ref[...] = cr * br - ci * bi
    bbi_ref[...] = cr * bi + ci * br


def _s5_prep(lam_re, lam_im, log_dt, b_re, b_im):
    wide = (DEPTH, GROUPS, STATE * GROUP_W)
    ldt = jnp.broadcast_to(log_dt[:, :, None], lam_re.shape)
    lam_w = jnp.repeat(jnp.stack([lam_re, lam_im, ldt]), GROUP_W, axis=-1)
    small = jax.ShapeDtypeStruct(lam_re.shape, F32)
    big = jax.ShapeDtypeStruct(wide, F32)
    return pl.pallas_call(
        _s5_prep_kernel, out_shape=(small, small, big, big), name="s5_prep",
    )(lam_re, lam_im, ldt, lam_w[0], lam_w[1], lam_w[2], b_re.reshape(wide), b_im.reshape(wide))


def _s5_matrices(bb_re, bb_im, c_re, c_im):
    bb = jnp.stack([bb_re, bb_im]).reshape(2, DEPTH, PAIRS, 2, STATE, GROUP_W)
    u = jnp.transpose(bb, (1, 2, 3, 5, 0, 4)).reshape(DEPTH, PAIRS, 2 * GROUP_W, 2 * STATE)
    spread = np.zeros((2, STATE, 2, 2, STATE), np.float32)
    for r in range(2):
        spread[r, np.arange(STATE), r, :, np.arange(STATE)] = 1.0
    t = jnp.einsum('ljrc,cd->ljrd', u, spread.reshape(2 * STATE, PAIR_W))
    same_group = (np.arange(2 * GROUP_W) // GROUP_W)[:, None] == ((np.arange(PAIR_W) // STATE) % 2)[None, :]
    t = jnp.where(same_group, t, 0.0)
    onehot = np.eye(4, dtype=np.float32)[np.arange(PAIRS) % 4].reshape(1, PAIRS, 4, 1, 1)
    bc = (onehot * t[:, :, None]).reshape(DEPTH, PAIRS, LANES, PAIR_W)
    cc = jnp.stack([c_re, -c_im]).reshape(2, DEPTH, C_TILES, 8, 2, GROUP_W, STATE)
    v = jnp.transpose(cc, (1, 2, 3, 0, 4, 6, 5)).reshape(DEPTH, C_TILES, C_K, GROUP_W)
    rows = np.arange(C_K)
    same_block = ((rows // PAIR_W) * 2 + (rows // STATE) % 2)[:, None] == (np.arange(2 * LANES) // GROUP_W)[None, :]
    replicate = np.tile(np.eye(GROUP_W, dtype=np.float32), (1, 2 * LANES // GROUP_W))
    cm = jnp.where(same_block, jnp.einsum('lnrh,hc->lnrc', v, replicate), 0.0)
    return bc.astype(BF16), cm.astype(BF16)


ADA_TILE = 1536


def _ada_kernel(c_ref, w_ref, b_ref, o_ref):
    c = c_ref[...]
    o_ref[...] = _bdot(c * _sigmoid(c), w_ref[...]) + b_ref[...]


def _ada(c_all, w_ada, b_ada):
    rows = c_all.shape[0]
    return pl.pallas_call(
        _ada_kernel,
        grid=(DEPTH, 6 * D // ADA_TILE),
        in_specs=[
            pl.BlockSpec((rows, D), lambda l, n: (0, 0)),
            pl.BlockSpec((None, D, ADA_TILE), lambda l, n: (l, 0, n)),
            pl.BlockSpec((None, 1, ADA_TILE), lambda l, n: (l, 0, n)),
        ],
        out_specs=pl.BlockSpec((None, rows, ADA_TILE), lambda l, n: (l, 0, n)),
        out_shape=jax.ShapeDtypeStruct((DEPTH, rows, 6 * D), F32),
        name="adaln",
    )(c_all, w_ada, b_ada.reshape(DEPTH, 1, 6 * D))


def _mod_specs(l, rows, row_block, which):
    return [pl.BlockSpec((None, rows, D), lambda *_, n=n: (l, row_block, n)) for n in which]


def _s5_input_map(x_slab, k, bc_ref, dst_ref, row0, rows):
    xb = x_slab.astype(BF16)
    for q in range(4):
        j = 4 * k + q
        dst_ref[row0:row0 + rows, j * PAIR_W:(j + 1) * PAIR_W] = jnp.dot(
            xb, bc_ref[j], preferred_element_type=F32)


def _s5_output_map(src_ref, row0, rows, cc_ref, n):
    hs = src_ref[row0:row0 + rows, n * C_K:(n + 1) * C_K]
    return jnp.dot(hs.astype(BF16), cc_ref[n], preferred_element_type=F32)


def _mix_back(xb, h, ps, y, mixed_fn, gt, win_ref, gv_ref, dsk_ref, wglu_ref, bglu_ref, wout_ref):
    v = _rms(_gelu(_bdot(h, win_ref[:, D:2 * D])), gv_ref[...])
    y_a = _gelu(_bdot(h, win_ref[:, 0:D])) * mixed_fn(v)
    merged = _sigmoid(_bdot(h, win_ref[:, 3 * D:4 * D])) * y_a
    z = _gelu(y + dsk_ref[...] * ps)
    y_b = z * _sigmoid(_bdot(z, wglu_ref[...]) + bglu_ref[...])
    merged = merged + _sigmoid(_bdot(h, win_ref[:, 4 * D:5 * D])) * y_b
    return xb + gt * _bdot(merged, wout_ref[...]), v


def _mix_prompt_kernel(x_ref, sh_ref, sc_ref, gt_ref, g1_ref, win_ref, gv_ref, wsp_ref, bsp_ref,
                       ar_ref, ai_ref, bc_ref, cc_ref, dsk_ref, wglu_ref, bglu_ref, wout_ref,
                       xo_ref, hre_ref, him_ref, ps_scr, y_scr, hs_scr, wt_scr):
    nb = x_ref.shape[0]
    assert nb == SUBLANES
    step = pl.program_id(0)

    @pl.when(step == 0)
    def _():
        hs_scr[0:SUBLANES, :] = jnp.zeros((SUBLANES, S_W), F32)

    row = lax.broadcasted_iota(jnp.int32, (CHUNK, CHUNK), 0)
    col = lax.broadcasted_iota(jnp.int32, (CHUNK, CHUNK), 1)
    for hd in range(HEADS):
        wt_scr[hd] = jnp.where(row >= col, wsp_ref[hd], 0.0).astype(BF16)

    def load_rows(c):
        xs, hs = [], []
        for bb in range(SEQ_PER_CHUNK):
            b = c * SEQ_PER_CHUNK + bb
            x = x_ref[b]
            xs.append(x)
            hs.append(_rms(x, g1_ref[...]) * (1.0 + sc_ref[pl.ds(b, 1), :]) + sh_ref[pl.ds(b, 1), :])
        return jnp.concatenate(xs, axis=0), jnp.concatenate(hs, axis=0).astype(BF16)

    def slab_rows(c, bb):
        return pl.ds(pl.multiple_of((c * SEQ_PER_CHUNK + bb) * PITCH, SUBLANES), SEQ_TILE)

    def phase_a(c, carry):
        _, h = load_rows(c)
        ps = _bdot(h, win_ref[:, 2 * D:3 * D])
        for bb in range(SEQ_PER_CHUNK):
            for k in range(SLABS):
                ps_scr[k, slab_rows(c, bb), :] = ps[bb * SEQ_TILE:(bb + 1) * SEQ_TILE, k * LANES:(k + 1) * LANES]
        return carry

    lax.fori_loop(0, nb // SEQ_PER_CHUNK, phase_a, 0)

    def phase_b(sub, carry):
        t0 = sub * SUB_T
        for k in range(SLABS):
            xk = jnp.concatenate(
                [ps_scr[k, pl.ds(t0 + t, SUBLANES, stride=PITCH), :] for t in range(SUB_T)], axis=0)
            _s5_input_map(xk, k, bc_ref, hs_scr, SUBLANES, SUB_T * SUBLANES)
        for j in range(PAIRS):
            re = slice(j * PAIR_W, j * PAIR_W + LANES)
            im = slice(j * PAIR_W + LANES, (j + 1) * PAIR_W)
            ar = jnp.broadcast_to(ar_ref[:, j * LANES:(j + 1) * LANES], (SUBLANES, LANES))
            ai = jnp.broadcast_to(ai_ref[:, j * LANES:(j + 1) * LANES], (SUBLANES, LANES))
            hr = hs_scr[0:SUBLANES, re]
            hi = hs_scr[0:SUBLANES, im]
            for t in range(SUB_T):
                r = slice(SUBLANES * (t + 1), SUBLANES * (t + 2))
                hr, hi = (ar * hr - ai * hi + hs_scr[r, re], ar * hi + ai * hr + hs_scr[r, im])
                hs_scr[r, re] = hr
                hs_scr[r, im] = hi
            hs_scr[0:SUBLANES, re] = hr
            hs_scr[0:SUBLANES, im] = hi
        for n in range(C_TILES):
            yn = _s5_output_map(hs_scr, SUBLANES, SUB_T * SUBLANES, cc_ref, n)
            for kk in range(2):
                for t in range(SUB_T):
                    y_scr[2 * n + kk, pl.ds(t0 + t, SUBLANES, stride=PITCH), :] = (
                        yn[t * SUBLANES:(t + 1) * SUBLANES, kk * LANES:(kk + 1) * LANES])
        return carry

    lax.fori_loop(0, SEQ_TILE // SUB_T, phase_b, 0)

    @pl.when(step == pl.num_programs(0) - 1)
    def _():
        for j in range(PAIRS):
            hre_ref[:, j * LANES:(j + 1) * LANES] = hs_scr[0:SUBLANES, j * PAIR_W:j * PAIR_W + LANES]
            him_ref[:, j * LANES:(j + 1) * LANES] = hs_scr[0:SUBLANES, j * PAIR_W + LANES:(j + 1) * PAIR_W]

    def mixed_fn(v):
        vb = v.astype(BF16)
        rows = []
        for bb in range(SEQ_PER_CHUNK):
            heads = [jnp.dot(wt_scr[hd], vb[bb * CHUNK:(bb + 1) * CHUNK, hd * HEAD_DIM:(hd + 1) * HEAD_DIM],
                             preferred_element_type=F32) for hd in range(HEADS)]
            rows.append(jnp.concatenate(heads, axis=1) + bsp_ref[...])
        return jnp.concatenate(rows, axis=0)

    def phase_c(c, carry):
        xb, h = load_rows(c)
        gather = lambda scr: jnp.concatenate(
            [jnp.concatenate([scr[k, slab_rows(c, bb), :] for k in range(SLABS)], axis=1)
             for bb in range(SEQ_PER_CHUNK)], axis=0)
        gt = jnp.concatenate(
            [jnp.broadcast_to(gt_ref[pl.ds(c * SEQ_PER_CHUNK + bb, 1), :], (SEQ_TILE, D))
             for bb in range(SEQ_PER_CHUNK)], axis=0)
        xn, _ = _mix_back(xb, h, gather(ps_scr), gather(y_scr), mixed_fn, gt,
                          win_ref, gv_ref, dsk_ref, wglu_ref, bglu_ref, wout_ref)
        for bb in range(SEQ_PER_CHUNK):
            xo_ref[c * SEQ_PER_CHUNK + bb] = xn[bb * SEQ_TILE:(bb + 1) * SEQ_TILE]
        return carry

    lax.fori_loop(0, nb // SEQ_PER_CHUNK, phase_c, 0)


def _mix_prompt(x, mod, mod_row_block, l, p):
    nb, seq, _ = x.shape
    assert nb == SUBLANES and seq % SEQ_TILE == 0
    tile = pl.BlockSpec((nb, SEQ_TILE, D), lambda s: (0, s, 0))
    consts = [p['g1'], p['w_in'], p['g_v'], p['w_sp'], p['b_sp'], p['a_re'], p['a_im'],
              p['bc'], p['cc'], p['d_skip'], p['w_glu'], p['b_glu'], p['w_out']]
    state = jax.ShapeDtypeStruct((nb, S_HALF), F32)
    state_spec = pl.BlockSpec((nb, S_HALF), lambda s: (0, 0))
    return pl.pallas_call(
        _mix_prompt_kernel,
        grid=(seq // SEQ_TILE,),
        in_specs=[tile] + _mod_specs(l, nb, mod_row_block, (0, 1, 2)) + [_layer_spec(a, l) for a in consts],
        out_specs=(tile, state_spec, state_spec),
        out_shape=(jax.ShapeDtypeStruct(x.shape, F32), state, state),
        scratch_shapes=[
            pltpu.VMEM((SLABS, nb * PITCH, LANES), F32),
            pltpu.VMEM((SLABS, nb * PITCH, LANES), F32),
            pltpu.VMEM((SUBLANES * (SUB_T + 1), S_W), F32),
            pltpu.VMEM((HEADS, CHUNK, CHUNK), BF16),
        ],
        compiler_params=pltpu.CompilerParams(
            dimension_semantics=("arbitrary",), vmem_limit_bytes=VMEM_LIMIT),
        name="mix_prompt",
    )(x, mod, mod, mod, *consts)


def _mix_sample_kernel(x_ref, sh_ref, sc_ref, gt_ref, g1_ref, win_ref, gv_ref, wd_ref, bd_ref,
                       ar_ref, ai_ref, bc_ref, cc_ref, dsk_ref, wglu_ref, bglu_ref, wout_ref,
                       h0r_ref, h0i_ref, xo_ref, hre_ref, him_ref, v_ref, hs_scr):
    rows = x_ref.shape[0]
    xb = x_ref[...]
    h = (_rms(xb, g1_ref[...]) * (1.0 + sc_ref[...]) + sh_ref[...]).astype(BF16)
    ps = _bdot(h, win_ref[:, 2 * D:3 * D])
    for k in range(SLABS):
        _s5_input_map(ps[:, k * LANES:(k + 1) * LANES], k, bc_ref, hs_scr, 0, rows)
    for j in range(PAIRS):
        half = slice(j * LANES, (j + 1) * LANES)
        re = slice(j * PAIR_W, j * PAIR_W + LANES)
        im = slice(j * PAIR_W + LANES, (j + 1) * PAIR_W)
        ar, ai = ar_ref[:, half], ai_ref[:, half]
        hr, hi = h0r_ref[:, half], h0i_ref[:, half]
        nr = ar * hr - ai * hi + hs_scr[:, re]
        ni = ar * hi + ai * hr + hs_scr[:, im]
        hs_scr[:, re] = nr
        hs_scr[:, im] = ni
        hre_ref[:, half] = nr
        him_ref[:, half] = ni
    y = jnp.concatenate([_s5_output_map(hs_scr, 0, rows, cc_ref, n) for n in range(C_TILES)], axis=1)
    mixed_fn = lambda v: wd_ref[...] * v + bd_ref[...]
    xn, v = _mix_back(xb, h, ps, y, mixed_fn, gt_ref[...], win_ref, gv_ref, dsk_ref, wglu_ref, bglu_ref, wout_ref)
    xo_ref[...] = xn
    v_ref[...] = v


def _mix_sample(x, mod, l, h0_re, h0_im, p):
    rows = x.shape[0]
    consts = [p['g1'], p['w_in'], p['g_v'], p['w_diag'], p['b_diag'], p['a_re'], p['a_im'],
              p['bc'], p['cc'], p['d_skip'], p['w_glu'], p['b_glu'], p['w_out'], h0_re, h0_im]
    whole = lambda shape: pl.BlockSpec(shape, lambda i: (0,) * len(shape))
    state = jax.ShapeDtypeStruct((rows, S_HALF), F32)
    return pl.pallas_call(
        _mix_sample_kernel,
        grid=(1,),
        in_specs=[whole((rows, D))] + _mod_specs(l, rows, 0, (0, 1, 2)) + [_layer_spec(a, l) for a in consts],
        out_specs=(whole((rows, D)), whole((rows, S_HALF)), whole((rows, S_HALF)), whole((rows, D))),
        out_shape=(jax.ShapeDtypeStruct((rows, D), F32), state, state, jax.ShapeDtypeStruct((rows, D), F32)),
        scratch_shapes=[pltpu.VMEM((rows, S_W), F32)],
        compiler_params=pltpu.CompilerParams(
            dimension_semantics=("arbitrary",), vmem_limit_bytes=VMEM_LIMIT),
        name="mix_sample",
    )(x, mod, mod, mod, *consts)


def _mlp_kernel(x_ref, sh_ref, sc_ref, gt_ref, g2_ref, w1_ref, w2_ref, gf_ref, xo_ref, *, final, tiles_per_mod):
    x = x_ref[...]
    if tiles_per_mod:
        b = pl.program_id(0) // tiles_per_mod
        sh, sc, gt = (r[pl.ds(b, 1), :] for r in (sh_ref, sc_ref, gt_ref))
    else:
        sh, sc, gt = sh_ref[...], sc_ref[...], gt_ref[...]
    h = (_rms(x, g2_ref[...]) * (1.0 + sc) + sh).astype(BF16)
    acc = jnp.zeros(x.shape, F32)
    for k in range(D_FF // D):
        a = jnp.dot(h, w1_ref[:, k * D:(k + 1) * D], preferred_element_type=F32)
        acc = acc + _bdot(jnp.square(jnp.maximum(a, 0.0)), w2_ref[k * D:(k + 1) * D, :])
    xn = x + gt * acc
    if final:
        xn = _rms(xn, gf_ref[...])
    xo_ref[...] = xn


def _mlp(x, mod, mod_row_block, l, p, g_final, final):
    if x.ndim == 3:
        nb, seq, _ = x.shape
        tm = min(MLP_ROWS, seq)
        assert seq % tm == 0
        per = seq // tm
        grid = (nb * per,)
        tile = pl.BlockSpec((None, tm, D), lambda i: (i // per, i % per, 0))
        mod_rows = nb
    else:
        per = 0
        grid = (1,)
        tile = pl.BlockSpec(x.shape, lambda i: (0, 0))
        mod_rows = x.shape[0]
    return pl.pallas_call(
        functools.partial(_mlp_kernel, final=final, tiles_per_mod=per),
        grid=grid,
        in_specs=[tile] + _mod_specs(l, mod_rows, mod_row_block, (3, 4, 5)) + [
            _layer_spec(p['g2'], l), _layer_spec(p['w_ff1'], l), _layer_spec(p['w_ff2'], l),
            pl.BlockSpec((1, D), lambda i: (0, 0))],
        out_specs=tile,
        out_shape=jax.ShapeDtypeStruct(x.shape, F32),
        compiler_params=pltpu.CompilerParams(
            dimension_semantics=("arbitrary",), vmem_limit_bytes=VMEM_LIMIT),
        name="mlp_final" if final else "mlp",
    )(x, mod, mod, mod, p['g2'], p['w_ff1'], p['w_ff2'], g_final)


def kernel(x_prompt, x_sample, c_prompt, c_sample, state_ssm_re, state_ssm_im, w_ada, b_ada, g_norm1, g_norm2, w_in, g_v, w_spatial, b_spatial, lam_re, lam_im, log_dt, b_re, b_im, c_re, c_im, d_skip, w_glu, b_glu, w_out, w_ff1, w_ff2, g_final):
    nbp, seq, _ = x_prompt.shape
    nbs = x_sample.shape[0]
    assert x_sample.shape[1] == 1 and nbs % nbp == 0

    a_re, a_im, bb_re, bb_im = _s5_prep(lam_re, lam_im, log_dt, b_re, b_im)
    bc, cc = _s5_matrices(bb_re, bb_im, c_re, c_im)
    mod = _ada(jnp.concatenate([c_sample, c_prompt], axis=0), w_ada.astype(BF16), b_ada)
    prompt_row_block = nbs // nbp

    vec = lambda a: a.reshape(DEPTH, 1, D)
    p = dict(
        g1=vec(g_norm1), g2=vec(g_norm2), g_v=vec(g_v), d_skip=vec(d_skip), b_glu=vec(b_glu),
        w_in=w_in.astype(BF16), w_glu=w_glu.astype(BF16), w_out=w_out.astype(BF16),
        w_ff1=w_ff1.astype(BF16), w_ff2=w_ff2.astype(BF16),
        w_sp=w_spatial, b_sp=jnp.repeat(jnp.swapaxes(b_spatial, 1, 2), HEAD_DIM, axis=2),
        w_diag=vec(jnp.repeat(w_spatial[:, :, 0, 0], HEAD_DIM, axis=1)),
        b_diag=vec(jnp.repeat(b_spatial[:, :, 0], HEAD_DIM, axis=1)),
        a_re=a_re.reshape(DEPTH, 1, S_HALF), a_im=a_im.reshape(DEPTH, 1, S_HALF), bc=bc, cc=cc)
    gf = g_final.reshape(1, D)
    h0_re = state_ssm_re.reshape(DEPTH, nbs, S_HALF)
    h0_im = state_ssm_im.reshape(DEPTH, nbs, S_HALF)

    xp = x_prompt
    xs = x_sample.reshape(nbs, D)
    re_p, im_p, re_s, im_s, v_s = [], [], [], [], []
    for l in range(DEPTH):
        final = l == DEPTH - 1
        xp, hr, hi = _mix_prompt(xp, mod, prompt_row_block, l, p)
        xp = _mlp(xp, mod, prompt_row_block, l, p, gf, final)
        re_p.append(hr)
        im_p.append(hi)
        xs, hr, hi, v = _mix_sample(xs, mod, l, h0_re, h0_im, p)
        xs = _mlp(xs, mod, 0, l, p, gf, final)
        re_s.append(hr)
        im_s.append(hi)
        v_s.append(v)

    state = lambda hs, nb: jnp.stack(hs).reshape(DEPTH, nb, GROUPS, STATE)
    return (xp, xs.reshape(nbs, 1, D), state(re_p, nbp), state(im_p, nbp),
            state(re_s, nbs), state(im_s, nbs), jnp.stack(v_s).reshape(DEPTH, nbs, 1, D))
```

```python
import functools
import math

import jax
import jax.numpy as jnp
import numpy as np
from jax import lax
from jax.experimental import pallas as pl
from jax.experimental.pallas import tpu as pltpu

F32 = jnp.float32
BF16 = jnp.bfloat16

D = 1024
DEPTH = 2
CHUNK = 128
HEADS = 4
HEAD_DIM = D // HEADS
GROUPS = 64
GROUP_W = 16
STATE = 64
D_FF = 4 * D
N_IN = 5 * D
EPS = 1e-6

LANES = 128
SUBLANES = 8
PAIRS = GROUPS // 2
PAIR_W = 4 * STATE
S_W = PAIRS * PAIR_W
S_HALF = S_W // 2
SLABS = D // LANES
SLAB_PAIRS = LANES // (2 * GROUP_W)
SLAB_W = SLAB_PAIRS * PAIR_W
CHUNK_T = 2

SEQ_TILE = CHUNK
ROW_CHUNK = 256
SEQ_PER_CHUNK = ROW_CHUNK // SEQ_TILE
SUB_C = 16
SUB_ROWS = SUB_C * SUBLANES
SUB_T = SUB_C * CHUNK_T
PITCH = SEQ_TILE + SUBLANES
MLP_ROWS = 512
VMEM_LIMIT = 61 * 1024 * 1024

_GELU_C = math.sqrt(2.0 / math.pi)


def _gelu(x):
    return x * (0.5 * (1.0 + jnp.tanh(_GELU_C * (x + 0.044715 * (x * x * x)))))


def _sigmoid(x):
    return 1.0 / (1.0 + jnp.exp(-x))


def _rms(x, g):
    return x * lax.rsqrt(jnp.mean(x * x, axis=-1, keepdims=True) + EPS) * g


def _bdot(a, b):
    return jnp.dot(a.astype(BF16), b, preferred_element_type=F32)


def _layer_spec(a, l):
    zeros = (0,) * (a.ndim - 1)
    return pl.BlockSpec((None,) + a.shape[1:], lambda *_: (l,) + zeros, pipeline_mode=pl.Buffered(1))


def _s5_prep_kernel(lr_ref, li_ref, ldt_ref, lrw_ref, liw_ref, ldtw_ref, br_ref, bi_ref,
                    ar_ref, ai_ref, bbr_ref, bbi_ref):
    def a_bar(lr, li, ldt):
        dt = jnp.exp(ldt)
        mag = jnp.exp(lr * dt)
        return mag * jnp.cos(li * dt), mag * jnp.sin(li * dt)

    ar, ai = a_bar(lr_ref[...], li_ref[...], ldt_ref[...])
    ar_ref[...] = ar
    ai_ref[...] = ai
    lr, li = lrw_ref[...], liw_ref[...]
    ar, ai = a_bar(lr, li, ldtw_ref[...])
    nr, ni = ar - 1.0, ai
    den = lr * lr + li * li
    cr = (nr * lr + ni * li) / den
    ci = (ni * lr - nr * li) / den
    br, bi = br_ref[...], bi_ref[...]
    bbr_ref[...] = cr * br - ci * bi
    bbi_ref[...] = cr * bi + ci * br


def _s5_prep(lam_re, lam_im, log_dt, b_re, b_im):
    wide = (DEPTH, GROUPS, STATE * GROUP_W)
    ldt = jnp.broadcast_to(log_dt[:, :, None], lam_re.shape)
    lam_w = jnp.repeat(jnp.stack([lam_re, lam_im, ldt]), GROUP_W, axis=-1)
    small = jax.ShapeDtypeStruct(lam_re.shape, F32)
    big = jax.ShapeDtypeStruct(wide, F32)
    return pl.pallas_call(
        _s5_prep_kernel, out_shape=(small, small, big, big), name="s5_prep",
    )(lam_re, lam_im, ldt, lam_w[0], lam_w[1], lam_w[2], b_re.reshape(wide), b_im.reshape(wide))


def _s5_slab_layout(a_re, a_im, bb_re, bb_im, c_re, c_im):
    hi = lax.Precision.HIGHEST
    bb = jnp.stack([bb_re, bb_im]).reshape(2, DEPTH, SLABS, SLAB_PAIRS, 2, STATE, GROUP_W)
    u = jnp.transpose(bb, (1, 2, 3, 4, 6, 0, 5)).reshape(DEPTH, SLABS, LANES, 2 * STATE)
    spread = np.zeros((2, STATE, SLAB_PAIRS, 2, 2, STATE), np.float32)
    for r in range(2):
        spread[r, np.arange(STATE), :, r, :, np.arange(STATE)] = 1.0
    t = jnp.einsum('lkrc,cd->lkrd', u, spread.reshape(2 * STATE, SLAB_W), precision=hi)
    lane = np.arange(SLAB_W)
    lane_group = (lane // PAIR_W) * 2 + (lane // STATE) % 2
    chan_group = np.arange(LANES) // GROUP_W
    b_slab = jnp.where(chan_group[:, None] == lane_group[None, :], t, 0.0)
    cc = jnp.stack([c_re, -c_im]).reshape(2, DEPTH, SLABS, SLAB_PAIRS, 2, GROUP_W, STATE)
    v = jnp.transpose(cc, (1, 2, 3, 0, 4, 6, 5)).reshape(DEPTH, SLABS, SLAB_W, GROUP_W)
    replicate = np.tile(np.eye(GROUP_W, dtype=np.float32), (1, LANES // GROUP_W))
    t = jnp.einsum('lkrh,hc->lkrc', v, replicate, precision=hi)
    c_slab = jnp.where(lane_group[:, None] == chan_group[None, :], t, 0.0)

    def lanes(a):
        a = a.reshape(DEPTH, SLABS, SLAB_PAIRS, 1, 2 * STATE)
        return jnp.broadcast_to(a, (DEPTH, SLABS, SLAB_PAIRS, 2, 2 * STATE)).reshape(DEPTH, SLABS, SLAB_W)

    return b_slab, c_slab, lanes(a_re), lanes(a_im)


def _cmul_cols(m, ar_row, ai_row):
    parts = []
    for q in range(SLAB_PAIRS):
        re = slice(q * PAIR_W, q * PAIR_W + LANES)
        im = slice(q * PAIR_W + LANES, (q + 1) * PAIR_W)
        ar, ai = ar_row[:, re], ai_row[:, re]
        parts += [m[:, re] * ar - m[:, im] * ai, m[:, im] * ar + m[:, re] * ai]
    return jnp.concatenate(parts, axis=1)


def _cmul_rows(m, ar_col, ai_col):
    parts = []
    for q in range(SLAB_PAIRS):
        re = slice(q * PAIR_W, q * PAIR_W + LANES)
        im = slice(q * PAIR_W + LANES, (q + 1) * PAIR_W)
        ar, ai = ar_col[re, :], ai_col[re, :]
        parts += [m[re, :] * ar + m[im, :] * ai, m[im, :] * ar - m[re, :] * ai]
    return jnp.concatenate(parts, axis=0)


def _s5_chunk_kernel(b_ref, c_ref, arr_ref, air_ref, arc_ref, aic_ref, wb_ref, wc_ref, wt_ref):
    b, c = b_ref[...], c_ref[...]
    ab = _cmul_cols(b, arr_ref[...], air_ref[...])
    wb_ref[0:LANES, :] = ab.astype(BF16)
    wb_ref[LANES:2 * LANES, :] = b.astype(BF16)
    ca = _cmul_rows(c, arc_ref[...], aic_ref[...])
    wc_ref[:, 0:LANES] = ca.astype(BF16)
    wc_ref[:, LANES:2 * LANES] = _cmul_rows(ca, arc_ref[...], aic_ref[...]).astype(BF16)
    k0 = jnp.dot(b, c, precision=lax.Precision.HIGHEST, preferred_element_type=F32).astype(BF16)
    k1 = jnp.dot(ab, c, precision=lax.Precision.HIGHEST, preferred_element_type=F32).astype(BF16)
    wt_ref[0:LANES, 0:LANES] = k0
    wt_ref[0:LANES, LANES:2 * LANES] = k1
    wt_ref[LANES:2 * LANES, 0:LANES] = jnp.zeros((LANES, LANES), BF16)
    wt_ref[LANES:2 * LANES, LANES:2 * LANES] = k0


def _s5_chunk_maps(b_slab, c_slab, a_re_lanes, a_im_lanes):
    row = lambda a: a.reshape(DEPTH, SLABS, 1, SLAB_W)
    col = lambda a: jnp.broadcast_to(a[..., None], (DEPTH, SLABS, SLAB_W, LANES))
    blk = lambda r, c: pl.BlockSpec((None, None, r, c), lambda l, k: (l, k, 0, 0))
    out = lambda r, c: jax.ShapeDtypeStruct((DEPTH, SLABS, r, c), BF16)
    return pl.pallas_call(
        _s5_chunk_kernel,
        grid=(DEPTH, SLABS),
        in_specs=[blk(LANES, SLAB_W), blk(SLAB_W, LANES), blk(1, SLAB_W), blk(1, SLAB_W),
                  blk(SLAB_W, LANES), blk(SLAB_W, LANES)],
        out_specs=(blk(2 * LANES, SLAB_W), blk(SLAB_W, 2 * LANES), blk(2 * LANES, 2 * LANES)),
        out_shape=(out(2 * LANES, SLAB_W), out(SLAB_W, 2 * LANES), out(2 * LANES, 2 * LANES)),
        name="s5_chunk_maps",
    )(b_slab, c_slab, row(a_re_lanes), row(a_im_lanes), col(a_re_lanes), col(a_im_lanes))


ADA_TILE = 1536


def _ada_kernel(c_ref, w_ref, b_ref, o_ref):
    c = c_ref[...]
    o_ref[...] = _bdot(c * _sigmoid(c), w_ref[...]) + b_ref[...]


def _ada(c_all, w_ada, b_ada):
    rows = c_all.shape[0]
    return pl.pallas_call(
        _ada_kernel,
        grid=(DEPTH, 6 * D // ADA_TILE),
        in_specs=[
            pl.BlockSpec((rows, D), lambda l, n: (0, 0)),
            pl.BlockSpec((None, D, ADA_TILE), lambda l, n: (l, 0, n)),
            pl.BlockSpec((None, 1, ADA_TILE), lambda l, n: (l, 0, n)),
        ],
        out_specs=pl.BlockSpec((None, rows, ADA_TILE), lambda l, n: (l, 0, n)),
        out_shape=jax.ShapeDtypeStruct((DEPTH, rows, 6 * D), F32),
        name="adaln",
    )(c_all, w_ada, b_ada.reshape(DEPTH, 1, 6 * D))


def _mod_specs(l, rows, row_block, which):
    return [pl.BlockSpec((None, rows, D), lambda *_, n=n: (l, row_block, n)) for n in which]


def _mix_back(xb, h, ps, y, mixed_fn, gt, win_ref, gv_ref, dsk_ref, wglu_ref, bglu_ref, wout_ref):
    v = _rms(_gelu(_bdot(h, win_ref[:, D:2 * D])), gv_ref[...])
    y_a = _gelu(_bdot(h, win_ref[:, 0:D])) * mixed_fn(v)
    merged = _sigmoid(_bdot(h, win_ref[:, 3 * D:4 * D])) * y_a
    z = _gelu(y + dsk_ref[...] * ps)
    y_b = z * _sigmoid(_bdot(z, wglu_ref[...]) + bglu_ref[...])
    merged = merged + _sigmoid(_bdot(h, win_ref[:, 4 * D:5 * D])) * y_b
    return xb + gt * _bdot(merged, wout_ref[...]), v


def _pair_lanes(j):
    re = slice(j * PAIR_W, j * PAIR_W + LANES)
    im = slice(j * PAIR_W + LANES, (j + 1) * PAIR_W)
    half = slice(j * LANES, (j + 1) * LANES)
    return re, im, half


def _mix_prompt_kernel(x_ref, sh_ref, sc_ref, gt_ref, g1_ref, win_ref, gv_ref, wsp_ref, bsp_ref,
                       ar_ref, ai_ref, wb_ref, wc_ref, wt_ref, dsk_ref, wglu_ref, bglu_ref, wout_ref,
                       xo_ref, hre_ref, him_ref, ps_scr, y_scr, hs_scr, xc_scr, tril_scr):
    nb = x_ref.shape[0]
    assert nb == SUBLANES
    step = pl.program_id(0)

    @pl.when(step == 0)
    def _():
        hs_scr[0:SUBLANES, :] = jnp.zeros((SUBLANES, S_W), F32)

    row = lax.broadcasted_iota(jnp.int32, (CHUNK, CHUNK), 0)
    col = lax.broadcasted_iota(jnp.int32, (CHUNK, CHUNK), 1)
    for hd in range(HEADS):
        tril_scr[hd] = jnp.where(row >= col, wsp_ref[hd], 0.0).astype(BF16)

    def load_rows(c):
        xs, hs = [], []
        for bb in range(SEQ_PER_CHUNK):
            b = c * SEQ_PER_CHUNK + bb
            x = x_ref[b]
            xs.append(x)
            hs.append(_rms(x, g1_ref[...]) * (1.0 + sc_ref[pl.ds(b, 1), :]) + sh_ref[pl.ds(b, 1), :])
        return jnp.concatenate(xs, axis=0), jnp.concatenate(hs, axis=0).astype(BF16)

    def slab_rows(c, bb):
        return pl.ds(pl.multiple_of((c * SEQ_PER_CHUNK + bb) * PITCH, SUBLANES), SEQ_TILE)

    def phase_a(c, carry):
        _, h = load_rows(c)
        ps = _bdot(h, win_ref[:, 2 * D:3 * D])
        for bb in range(SEQ_PER_CHUNK):
            for k in range(SLABS):
                ps_scr[k, slab_rows(c, bb), :] = ps[bb * SEQ_TILE:(bb + 1) * SEQ_TILE, k * LANES:(k + 1) * LANES]
        return carry

    lax.fori_loop(0, nb // SEQ_PER_CHUNK, phase_a, 0)

    def phase_b(sub, carry):
        t0 = sub * SUB_T

        def step_rows(k, i):
            return [pl.ds(t0 + CHUNK_T * c + i, SUBLANES, stride=PITCH) for c in range(SUB_C)]

        for k in range(SLABS):
            xc = jnp.concatenate(
                [jnp.concatenate([ps_scr[k, r, :] for r in step_rows(k, i)], axis=0) for i in range(CHUNK_T)],
                axis=1).astype(BF16)
            xc_scr[k] = xc
            hs_scr[SUBLANES:SUBLANES + SUB_ROWS, k * SLAB_W:(k + 1) * SLAB_W] = jnp.dot(
                xc, wb_ref[k], preferred_element_type=F32)
        for j in range(PAIRS):
            re, im, half = _pair_lanes(j)
            ar = jnp.broadcast_to(ar_ref[:, half], (SUBLANES, LANES))
            ai = jnp.broadcast_to(ai_ref[:, half], (SUBLANES, LANES))
            ar, ai = ar * ar - ai * ai, 2.0 * (ar * ai)
            hr = hs_scr[0:SUBLANES, re]
            hi = hs_scr[0:SUBLANES, im]
            for c in range(SUB_C):
                r = slice(SUBLANES * (c + 1), SUBLANES * (c + 2))
                hr, hi = (ar * hr - ai * hi + hs_scr[r, re], ar * hi + ai * hr + hs_scr[r, im])
                hs_scr[r, re] = hr
                hs_scr[r, im] = hi
        for k in range(SLABS):
            h_prev = hs_scr[0:SUB_ROWS, k * SLAB_W:(k + 1) * SLAB_W]
            y2 = (jnp.dot(xc_scr[k], wt_ref[k], preferred_element_type=F32)
                  + jnp.dot(h_prev.astype(BF16), wc_ref[k], preferred_element_type=F32))
            for i in range(CHUNK_T):
                for c, r in enumerate(step_rows(k, i)):
                    y_scr[k, r, :] = y2[c * SUBLANES:(c + 1) * SUBLANES, i * LANES:(i + 1) * LANES]
        hs_scr[0:SUBLANES, :] = hs_scr[SUB_ROWS:SUB_ROWS + SUBLANES, :]
        return carry

    lax.fori_loop(0, SEQ_TILE // SUB_T, phase_b, 0)

    @pl.when(step == pl.num_programs(0) - 1)
    def _():
        for j in range(PAIRS):
            re, im, half = _pair_lanes(j)
            hre_ref[:, half] = hs_scr[0:SUBLANES, re]
            him_ref[:, half] = hs_scr[0:SUBLANES, im]

    def mixed_fn(v):
        vb = v.astype(BF16)
        rows = []
        for bb in range(SEQ_PER_CHUNK):
            heads = [jnp.dot(tril_scr[hd], vb[bb * CHUNK:(bb + 1) * CHUNK, hd * HEAD_DIM:(hd + 1) * HEAD_DIM],
                             preferred_element_type=F32) for hd in range(HEADS)]
            rows.append(jnp.concatenate(heads, axis=1) + bsp_ref[...])
        return jnp.concatenate(rows, axis=0)

    def phase_c(c, carry):
        xb, h = load_rows(c)
        gather = lambda scr: jnp.concatenate(
            [jnp.concatenate([scr[k, slab_rows(c, bb), :] for k in range(SLABS)], axis=1)
             for bb in range(SEQ_PER_CHUNK)], axis=0)
        gt = jnp.concatenate(
            [jnp.broadcast_to(gt_ref[pl.ds(c * SEQ_PER_CHUNK + bb, 1), :], (SEQ_TILE, D))
             for bb in range(SEQ_PER_CHUNK)], axis=0)
        xn, _ = _mix_back(xb, h, gather(ps_scr), gather(y_scr), mixed_fn, gt,
                          win_ref, gv_ref, dsk_ref, wglu_ref, bglu_ref, wout_ref)
        for bb in range(SEQ_PER_CHUNK):
            xo_ref[c * SEQ_PER_CHUNK + bb] = xn[bb * SEQ_TILE:(bb + 1) * SEQ_TILE]
        return carry

    lax.fori_loop(0, nb // SEQ_PER_CHUNK, phase_c, 0)


def _mix_prompt(x, mod, mod_row_block, l, p):
    nb, seq, _ = x.shape
    assert nb == SUBLANES and seq % SEQ_TILE == 0
    tile = pl.BlockSpec((nb, SEQ_TILE, D), lambda s: (0, s, 0))
    consts = [p['g1'], p['w_in'], p['g_v'], p['w_sp'], p['b_sp'], p['a_re'], p['a_im'],
              p['wb'], p['wc'], p['wt'], p['d_skip'], p['w_glu'], p['b_glu'], p['w_out']]
    state = jax.ShapeDtypeStruct((nb, S_HALF), F32)
    state_spec = pl.BlockSpec((nb, S_HALF), lambda s: (0, 0))
    return pl.pallas_call(
        _mix_prompt_kernel,
        grid=(seq // SEQ_TILE,),
        in_specs=[tile] + _mod_specs(l, nb, mod_row_block, (0, 1, 2)) + [_layer_spec(a, l) for a in consts],
        out_specs=(tile, state_spec, state_spec),
        out_shape=(jax.ShapeDtypeStruct(x.shape, F32), state, state),
        scratch_shapes=[
            pltpu.VMEM((SLABS, nb * PITCH, LANES), F32),
            pltpu.VMEM((SLABS, nb * PITCH, LANES), F32),
            pltpu.VMEM((SUB_ROWS + SUBLANES, S_W), F32),
            pltpu.VMEM((SLABS, SUB_ROWS, CHUNK_T * LANES), BF16),
            pltpu.VMEM((HEADS, CHUNK, CHUNK), BF16),
        ],
        compiler_params=pltpu.CompilerParams(
            dimension_semantics=("arbitrary",), vmem_limit_bytes=VMEM_LIMIT),
        name="mix_prompt",
    )(x, mod, mod, mod, *consts)


def _mix_sample_kernel(x_ref, sh_ref, sc_ref, gt_ref, g1_ref, win_ref, gv_ref, wd_ref, bd_ref,
                       ar_ref, ai_ref, wb_ref, wc_ref, wt_ref, dsk_ref, wglu_ref, bglu_ref, wout_ref,
                       h0r_ref, h0i_ref, xo_ref, hre_ref, him_ref, v_ref):
    xb = x_ref[...]
    h = (_rms(xb, g1_ref[...]) * (1.0 + sc_ref[...]) + sh_ref[...]).astype(BF16)
    ps = _bdot(h, win_ref[:, 2 * D:3 * D])
    ys = []
    for k in range(SLABS):
        xk = ps[:, k * LANES:(k + 1) * LANES].astype(BF16)
        bu = jnp.dot(xk, wb_ref[k, LANES:2 * LANES, :], preferred_element_type=F32)
        h0 = []
        for q in range(SLAB_PAIRS):
            re, im, half = _pair_lanes(q)
            _, _, half = _pair_lanes(k * SLAB_PAIRS + q)
            ar, ai = ar_ref[:, half], ai_ref[:, half]
            hr, hi = h0r_ref[:, half], h0i_ref[:, half]
            hre_ref[:, half] = ar * hr - ai * hi + bu[:, re]
            him_ref[:, half] = ar * hi + ai * hr + bu[:, im]
            h0 += [hr, hi]
        ys.append(_bdot(jnp.concatenate(h0, axis=1), wc_ref[k, :, 0:LANES])
                  + jnp.dot(xk, wt_ref[k, 0:LANES, 0:LANES], preferred_element_type=F32))
    y = jnp.concatenate(ys, axis=1)
    mixed_fn = lambda v: wd_ref[...] * v + bd_ref[...]
    xn, v = _mix_back(xb, h, ps, y, mixed_fn, gt_ref[...], win_ref, gv_ref, dsk_ref, wglu_ref, bglu_ref, wout_ref)
    xo_ref[...] = xn
    v_ref[...] = v


def _mix_sample(x, mod, l, h0_re, h0_im, p):
    rows = x.shape[0]
    consts = [p['g1'], p['w_in'], p['g_v'], p['w_diag'], p['b_diag'], p['a_re'], p['a_im'],
              p['wb'], p['wc'], p['wt'], p['d_skip'], p['w_glu'], p['b_glu'], p['w_out'], h0_re, h0_im]
    whole = lambda shape: pl.BlockSpec(shape, lambda i: (0,) * len(shape))
    state = jax.ShapeDtypeStruct((rows, S_HALF), F32)
    return pl.pallas_call(
        _mix_sample_kernel,
        grid=(1,),
        in_specs=[whole((rows, D))] + _mod_specs(l, rows, 0, (0, 1, 2)) + [_layer_spec(a, l) for a in consts],
        out_specs=(whole((rows, D)), whole((rows, S_HALF)), whole((rows, S_HALF)), whole((rows, D))),
        out_shape=(jax.ShapeDtypeStruct((rows, D), F32), state, state, jax.ShapeDtypeStruct((rows, D), F32)),
        compiler_params=pltpu.CompilerParams(
            dimension_semantics=("arbitrary",), vmem_limit_bytes=VMEM_LIMIT),
        name="mix_sample",
    )(x, mod, mod, mod, *consts)


def _mlp_kernel(x_ref, sh_ref, sc_ref, gt_ref, g2_ref, w1_ref, w2_ref, gf_ref, xo_ref, *, final, tiles_per_mod):
    x = x_ref[...]
    if tiles_per_mod:
        b = pl.program_id(0) // tiles_per_mod
        sh, sc, gt = (r[pl.ds(b, 1), :] for r in (sh_ref, sc_ref, gt_ref))
    else:
        sh, sc, gt = sh_ref[...], sc_ref[...], gt_ref[...]
    h = (_rms(x, g2_ref[...]) * (1.0 + sc) + sh).astype(BF16)
    acc = jnp.zeros(x.shape, F32)
    for k in range(D_FF // D):
        a = jnp.dot(h, w1_ref[:, k * D:(k + 1) * D], preferred_element_type=F32)
        acc = acc + _bdot(jnp.square(jnp.maximum(a, 0.0)), w2_ref[k * D:(k + 1) * D, :])
    xn = x + gt * acc
    if final:
        xn = _rms(xn, gf_ref[...])
    xo_ref[...] = xn


def _mlp(x, mod, mod_row_block, l, p, g_final, final):
    if x.ndim == 3:
        nb, seq, _ = x.shape
        tm = min(MLP_ROWS, seq)
        assert seq % tm == 0
        per = seq // tm
        grid = (nb * per,)
        tile = pl.BlockSpec((None, tm, D), lambda i: (i // per, i % per, 0))
        mod_rows = nb
    else:
        per = 0
        grid = (1,)
        tile = pl.BlockSpec(x.shape, lambda i: (0, 0))
        mod_rows = x.shape[0]
    return pl.pallas_call(
        functools.partial(_mlp_kernel, final=final, tiles_per_mod=per),
        grid=grid,
        in_specs=[tile] + _mod_specs(l, mod_rows, mod_row_block, (3, 4, 5)) + [
            _layer_spec(p['g2'], l), _layer_spec(p['w_ff1'], l), _layer_spec(p['w_ff2'], l),
            pl.BlockSpec((1, D), lambda i: (0, 0))],
        out_specs=tile,
        out_shape=jax.ShapeDtypeStruct(x.shape, F32),
        compiler_params=pltpu.CompilerParams(
            dimension_semantics=("arbitrary",), vmem_limit_bytes=VMEM_LIMIT),
        name="mlp_final" if final else "mlp",
    )(x, mod, mod, mod, p['g2'], p['w_ff1'], p['w_ff2'], g_final)


def kernel(x_prompt, x_sample, c_prompt, c_sample, state_ssm_re, state_ssm_im, w_ada, b_ada, g_norm1, g_norm2, w_in, g_v, w_spatial, b_spatial, lam_re, lam_im, log_dt, b_re, b_im, c_re, c_im, d_skip, w_glu, b_glu, w_out, w_ff1, w_ff2, g_final):
    nbp, seq, _ = x_prompt.shape
    nbs = x_sample.shape[0]
    assert x_sample.shape[1] == 1 and nbs % nbp == 0

    a_re, a_im, bb_re, bb_im = _s5_prep(lam_re, lam_im, log_dt, b_re, b_im)
    wb, wc, wt = _s5_chunk_maps(*_s5_slab_layout(a_re, a_im, bb_re, bb_im, c_re, c_im))
    mod = _ada(jnp.concatenate([c_sample, c_prompt], axis=0), w_ada.astype(BF16), b_ada)
    prompt_row_block = nbs // nbp

    vec = lambda a: a.reshape(DEPTH, 1, D)
    p = dict(
        g1=vec(g_norm1), g2=vec(g_norm2), g_v=vec(g_v), d_skip=vec(d_skip), b_glu=vec(b_glu),
        w_in=w_in.astype(BF16), w_glu=w_glu.astype(BF16), w_out=w_out.astype(BF16),
        w_ff1=w_ff1.astype(BF16), w_ff2=w_ff2.astype(BF16),
        w_sp=w_spatial, b_sp=jnp.repeat(jnp.swapaxes(b_spatial, 1, 2), HEAD_DIM, axis=2),
        w_diag=vec(jnp.repeat(w_spatial[:, :, 0, 0], HEAD_DIM, axis=1)),
        b_diag=vec(jnp.repeat(b_spatial[:, :, 0], HEAD_DIM, axis=1)),
        a_re=a_re.reshape(DEPTH, 1, S_HALF), a_im=a_im.reshape(DEPTH, 1, S_HALF), wb=wb, wc=wc, wt=wt)
    gf = g_final.reshape(1, D)
    h0_re = state_ssm_re.reshape(DEPTH, nbs, S_HALF)
    h0_im = state_ssm_im.reshape(DEPTH, nbs, S_HALF)

    xp = x_prompt
    xs = x_sample.reshape(nbs, D)
    re_p, im_p, re_s, im_s, v_s = [], [], [], [], []
    for l in range(DEPTH):
        final = l == DEPTH - 1
        xp, hr, hi = _mix_prompt(xp, mod, prompt_row_block, l, p)
        xp = _mlp(xp, mod, prompt_row_block, l, p, gf, final)
        re_p.append(hr)
        im_p.append(hi)
        xs, hr, hi, v = _mix_sample(xs, mod, l, h0_re, h0_im, p)
        xs = _mlp(xs, mod, 0, l, p, gf, final)
        re_s.append(hr)
        im_s.append(hi)
        v_s.append(v)

    state = lambda hs, nb: jnp.stack(hs).reshape(DEPTH, nb, GROUPS, STATE)
    return (xp, xs.reshape(nbs, 1, D), state(re_p, nbp), state(im_p, nbp),
            state(re_s, nbs), state(im_s, nbs), jnp.stack(v_s).reshape(DEPTH, nbs, 1, D))
```

```python
import functools
import math

import jax
import jax.numpy as jnp
import numpy as np
from jax import lax
from jax.experimental import pallas as pl
from jax.experimental.pallas import tpu as pltpu

F32 = jnp.float32
BF16 = jnp.bfloat16

D = 1024
DEPTH = 2
CHUNK = 128
HEADS = 4
HEAD_DIM = D // HEADS
GROUPS = 64
GROUP_W = 16
STATE = 64
D_FF = 4 * D
N_IN = 5 * D
EPS = 1e-6

LANES = 128
SUBLANES = 8
PAIRS = GROUPS // 2
PAIR_W = 4 * STATE
S_W = PAIRS * PAIR_W
S_HALF = S_W // 2
SLABS = D // LANES
SLAB_PAIRS = LANES // (2 * GROUP_W)
SLAB_W = SLAB_PAIRS * PAIR_W
CHUNK_T = 2

SEQ_TILE = CHUNK
ROW_CHUNK = 256
SEQ_PER_CHUNK = ROW_CHUNK // SEQ_TILE
SUB_C = 16
SUB_ROWS = SUB_C * SUBLANES
SUB_T = SUB_C * CHUNK_T
PITCH = SEQ_TILE + SUBLANES
MLP_ROWS = 512
VMEM_LIMIT = 61 * 1024 * 1024

_GELU_C = math.sqrt(2.0 / math.pi)


def _gelu(x):
    return x * (0.5 * (1.0 + jnp.tanh(_GELU_C * (x + 0.044715 * (x * x * x)))))


def _sigmoid(x):
    return 1.0 / (1.0 + jnp.exp(-x))


def _rms(x, g):
    return x * lax.rsqrt(jnp.mean(x * x, axis=-1, keepdims=True) + EPS) * g


def _bdot(a, b):
    return jnp.dot(a.astype(BF16), b, preferred_element_type=F32)


def _layer_spec(a, l):
    zeros = (0,) * (a.ndim - 1)
    return pl.BlockSpec((None,) + a.shape[1:], lambda *_: (l,) + zeros, pipeline_mode=pl.Buffered(1))


def _s5_prep_kernel(lr_ref, li_ref, ldt_ref, lrw_ref, liw_ref, ldtw_ref, br_ref, bi_ref,
                    ar_ref, ai_ref, bbr_ref, bbi_ref):
    def a_bar(lr, li, ldt):
        dt = jnp.exp(ldt)
        mag = jnp.exp(lr * dt)
        return mag * jnp.cos(li * dt), mag * jnp.sin(li * dt)

    ar, ai = a_bar(lr_ref[...], li_ref[...], ldt_ref[...])
    ar_ref[...] = ar
    ai_ref[...] = ai
    lr, li = lrw_ref[...], liw_ref[...]
    ar, ai = a_bar(lr, li, ldtw_ref[...])
    nr, ni = ar - 1.0, ai
    den = lr * lr + li * li
    cr = (nr * lr + ni * li) / den
    ci = (ni * lr - nr * li) / den
    br, bi = br_ref[...], bi_ref[...]
    bbr_ref[...] = cr * br - ci * bi
    bbi_ref[...] = cr * bi + ci * br


def _s5_prep(lam_re, lam_im, log_dt, b_re, b_im):
    wide = (DEPTH, GROUPS, STATE * GROUP_W)
    ldt = jnp.broadcast_to(log_dt[:, :, None], lam_re.shape)
    lam_w = jnp.repeat(jnp.stack([lam_re, lam_im, ldt]), GROUP_W, axis=-1)
    small = jax.ShapeDtypeStruct(lam_re.shape, F32)
    big = jax.ShapeDtypeStruct(wide, F32)
    return pl.pallas_call(
        _s5_prep_kernel, out_shape=(small, small, big, big), name="s5_prep",
    )(lam_re, lam_im, ldt, lam_w[0], lam_w[1], lam_w[2], b_re.reshape(wide), b_im.reshape(wide))


def _s5_slab_layout(a_re, a_im, bb_re, bb_im, c_re, c_im):
    bb = jnp.stack([bb_re, bb_im]).reshape(2, DEPTH, SLABS, SLAB_PAIRS, 2, STATE, GROUP_W)
    b = jnp.transpose(bb, (1, 2, 3, 4, 6, 0, 5)).reshape(DEPTH, SLABS, LANES, 2 * STATE)
    cc = jnp.stack([c_re, -c_im]).reshape(2, DEPTH, SLABS, SLAB_PAIRS, 2, GROUP_W, STATE)
    c = jnp.transpose(cc, (1, 2, 3, 0, 4, 6, 5)).reshape(DEPTH, SLABS, SLAB_W, GROUP_W)

    def lanes(a):
        a = a.reshape(DEPTH, SLABS, SLAB_PAIRS, 1, 2 * STATE)
        return jnp.broadcast_to(a, (DEPTH, SLABS, SLAB_PAIRS, 2, 2 * STATE)).reshape(DEPTH, SLABS, SLAB_W)

    return b, c, lanes(a_re), lanes(a_im)


def _replication_matrices():
    spread = np.zeros((2, STATE, SLAB_PAIRS, 2, 2, STATE), np.float32)
    for r in range(2):
        spread[r, np.arange(STATE), :, r, :, np.arange(STATE)] = 1.0
    replicate = np.tile(np.eye(GROUP_W, dtype=np.float32), (1, LANES // GROUP_W))
    return jnp.asarray(spread.reshape(2 * STATE, SLAB_W), BF16), jnp.asarray(replicate, BF16)


def _same_group(shape, lane_axis):
    lane = lax.broadcasted_iota(jnp.int32, shape, lane_axis)
    chan = lax.broadcasted_iota(jnp.int32, shape, 1 - lane_axis)
    lane_group = (lane // PAIR_W) * 2 + (lane // STATE) % 2
    return lane_group == chan // GROUP_W


def _cmul_cols(m, ar_row, ai_row):
    parts = []
    for q in range(SLAB_PAIRS):
        re = slice(q * PAIR_W, q * PAIR_W + LANES)
        im = slice(q * PAIR_W + LANES, (q + 1) * PAIR_W)
        ar, ai = ar_row[:, re], ai_row[:, re]
        parts += [m[:, re] * ar - m[:, im] * ai, m[:, im] * ar + m[:, re] * ai]
    return jnp.concatenate(parts, axis=1)


def _cmul_rows(m, ar_col, ai_col):
    parts = []
    for q in range(SLAB_PAIRS):
        re = slice(q * PAIR_W, q * PAIR_W + LANES)
        im = slice(q * PAIR_W + LANES, (q + 1) * PAIR_W)
        ar, ai = ar_col[re, :], ai_col[re, :]
        parts += [m[re, :] * ar + m[im, :] * ai, m[im, :] * ar - m[re, :] * ai]
    return jnp.concatenate(parts, axis=0)


def _s5_chunk_kernel(b_ref, c_ref, spread_ref, rep_ref, arr_ref, air_ref, arc_ref, aic_ref, wb_ref, wc_ref, wt_ref):
    b = jnp.where(_same_group((LANES, SLAB_W), 1), _bdot(b_ref[...], spread_ref[...]), 0.0)
    c = jnp.where(_same_group((SLAB_W, LANES), 0), _bdot(c_ref[...], rep_ref[...]), 0.0)
    ab = _cmul_cols(b, arr_ref[...], air_ref[...])
    wb_ref[0:LANES, :] = ab.astype(BF16)
    wb_ref[LANES:2 * LANES, :] = b.astype(BF16)
    ca = _cmul_rows(c, arc_ref[...], aic_ref[...])
    wc_ref[:, 0:LANES] = ca.astype(BF16)
    wc_ref[:, LANES:2 * LANES] = _cmul_rows(ca, arc_ref[...], aic_ref[...]).astype(BF16)
    cb = c.astype(BF16)
    k0 = _bdot(b, cb).astype(BF16)
    wt_ref[0:LANES, 0:LANES] = k0
    wt_ref[0:LANES, LANES:2 * LANES] = _bdot(ab, cb).astype(BF16)
    wt_ref[LANES:2 * LANES, 0:LANES] = jnp.zeros((LANES, LANES), BF16)
    wt_ref[LANES:2 * LANES, LANES:2 * LANES] = k0


def _s5_chunk_maps(b, c, a_re_lanes, a_im_lanes):
    spread, replicate = _replication_matrices()
    row = lambda a: a.reshape(DEPTH, SLABS, 1, SLAB_W)
    col = lambda a: a.reshape(DEPTH, SLABS, SLAB_W, 1)
    blk = lambda r, c: pl.BlockSpec((None, None, r, c), lambda l, k: (l, k, 0, 0))
    whole = lambda a: pl.BlockSpec(a.shape, lambda l, k: (0, 0))
    out = lambda r, c: jax.ShapeDtypeStruct((DEPTH, SLABS, r, c), BF16)
    return pl.pallas_call(
        _s5_chunk_kernel,
        grid=(DEPTH, SLABS),
        in_specs=[blk(LANES, 2 * STATE), blk(SLAB_W, GROUP_W), whole(spread), whole(replicate),
                  blk(1, SLAB_W), blk(1, SLAB_W), blk(SLAB_W, 1), blk(SLAB_W, 1)],
        out_specs=(blk(2 * LANES, SLAB_W), blk(SLAB_W, 2 * LANES), blk(2 * LANES, 2 * LANES)),
        out_shape=(out(2 * LANES, SLAB_W), out(SLAB_W, 2 * LANES), out(2 * LANES, 2 * LANES)),
        name="s5_chunk_maps",
    )(b, c, spread, replicate, row(a_re_lanes), row(a_im_lanes), col(a_re_lanes), col(a_im_lanes))


ADA_TILE = 1536


def _ada_kernel(c_ref, w_ref, b_ref, o_ref):
    c = c_ref[...]
    o_ref[...] = _bdot(c * _sigmoid(c), w_ref[...].astype(BF16)) + b_ref[...]


def _ada(c_all, w_ada, b_ada):
    rows = c_all.shape[0]
    return pl.pallas_call(
        _ada_kernel,
        grid=(DEPTH, 6 * D // ADA_TILE),
        in_specs=[
            pl.BlockSpec((rows, D), lambda l, n: (0, 0)),
            pl.BlockSpec((None, D, ADA_TILE), lambda l, n: (l, 0, n)),
            pl.BlockSpec((None, 1, ADA_TILE), lambda l, n: (l, 0, n)),
        ],
        out_specs=pl.BlockSpec((None, rows, ADA_TILE), lambda l, n: (l, 0, n)),
        out_shape=jax.ShapeDtypeStruct((DEPTH, rows, 6 * D), F32),
        name="adaln",
    )(c_all, w_ada, b_ada.reshape(DEPTH, 1, 6 * D))


def _mod_specs(l, rows, row_block, which):
    return [pl.BlockSpec((None, rows, D), lambda *_, n=n: (l, row_block, n)) for n in which]


def _mix_back(xb, h, ps, y, mixed_fn, gt, win_ref, gv_ref, dsk_ref, wglu_ref, bglu_ref, wout_ref):
    v = _rms(_gelu(_bdot(h, win_ref[:, D:2 * D])), gv_ref[...])
    y_a = _gelu(_bdot(h, win_ref[:, 0:D])) * mixed_fn(v)
    merged = _sigmoid(_bdot(h, win_ref[:, 3 * D:4 * D])) * y_a
    z = _gelu(y + dsk_ref[...] * ps)
    y_b = z * _sigmoid(_bdot(z, wglu_ref[...]) + bglu_ref[...])
    merged = merged + _sigmoid(_bdot(h, win_ref[:, 4 * D:5 * D])) * y_b
    return xb + gt * _bdot(merged, wout_ref[...]), v


def _pair_lanes(j):
    re = slice(j * PAIR_W, j * PAIR_W + LANES)
    im = slice(j * PAIR_W + LANES, (j + 1) * PAIR_W)
    half = slice(j * LANES, (j + 1) * LANES)
    return re, im, half


def _mix_prompt_kernel(x_ref, sh_ref, sc_ref, gt_ref, g1_ref, win_ref, gv_ref, wsp_ref, bsp_ref,
                       ar_ref, ai_ref, wb_ref, wc_ref, wt_ref, dsk_ref, wglu_ref, bglu_ref, wout_ref,
                       xo_ref, hre_ref, him_ref, ps_scr, y_scr, hs_scr, xc_scr, tril_scr):
    nb = x_ref.shape[0]
    assert nb == SUBLANES
    step = pl.program_id(0)

    @pl.when(step == 0)
    def _():
        hs_scr[0:SUBLANES, :] = jnp.zeros((SUBLANES, S_W), F32)

    row = lax.broadcasted_iota(jnp.int32, (CHUNK, CHUNK), 0)
    col = lax.broadcasted_iota(jnp.int32, (CHUNK, CHUNK), 1)
    for hd in range(HEADS):
        tril_scr[hd] = jnp.where(row >= col, wsp_ref[hd], 0.0).astype(BF16)

    def load_rows(c):
        xs, hs = [], []
        for bb in range(SEQ_PER_CHUNK):
            b = c * SEQ_PER_CHUNK + bb
            x = x_ref[b]
            xs.append(x)
            hs.append(_rms(x, g1_ref[...]) * (1.0 + sc_ref[pl.ds(b, 1), :]) + sh_ref[pl.ds(b, 1), :])
        return jnp.concatenate(xs, axis=0), jnp.concatenate(hs, axis=0).astype(BF16)

    def slab_rows(c, bb):
        return pl.ds(pl.multiple_of((c * SEQ_PER_CHUNK + bb) * PITCH, SUBLANES), SEQ_TILE)

    def phase_a(c, carry):
        _, h = load_rows(c)
        ps = _bdot(h, win_ref[:, 2 * D:3 * D])
        for bb in range(SEQ_PER_CHUNK):
            for k in range(SLABS):
                ps_scr[k, slab_rows(c, bb), :] = ps[bb * SEQ_TILE:(bb + 1) * SEQ_TILE, k * LANES:(k + 1) * LANES]
        return carry

    lax.fori_loop(0, nb // SEQ_PER_CHUNK, phase_a, 0, unroll=True)

    def phase_b(sub, carry):
        t0 = sub * SUB_T

        def step_rows(k, i):
            return [pl.ds(t0 + CHUNK_T * c + i, SUBLANES, stride=PITCH) for c in range(SUB_C)]

        for k in range(SLABS):
            xc = jnp.concatenate(
                [jnp.concatenate([ps_scr[k, r, :] for r in step_rows(k, i)], axis=0) for i in range(CHUNK_T)],
                axis=1).astype(BF16)
            xc_scr[k] = xc
            hs_scr[SUBLANES:SUBLANES + SUB_ROWS, k * SLAB_W:(k + 1) * SLAB_W] = jnp.dot(
                xc, wb_ref[k], preferred_element_type=F32)
        for j in range(PAIRS):
            re, im, half = _pair_lanes(j)
            ar = jnp.broadcast_to(ar_ref[:, half], (SUBLANES, LANES))
            ai = jnp.broadcast_to(ai_ref[:, half], (SUBLANES, LANES))
            ar, ai = ar * ar - ai * ai, 2.0 * (ar * ai)
            hr = hs_scr[0:SUBLANES, re]
            hi = hs_scr[0:SUBLANES, im]
            for c in range(SUB_C):
                r = slice(SUBLANES * (c + 1), SUBLANES * (c + 2))
                hr, hi = (ar * hr - ai * hi + hs_scr[r, re], ar * hi + ai * hr + hs_scr[r, im])
                hs_scr[r, re] = hr
                hs_scr[r, im] = hi
        for k in range(SLABS):
            h_prev = hs_scr[0:SUB_ROWS, k * SLAB_W:(k + 1) * SLAB_W]
            y2 = (jnp.dot(xc_scr[k], wt_ref[k], preferred_element_type=F32)
                  + jnp.dot(h_prev.astype(BF16), wc_ref[k], preferred_element_type=F32))
            for i in range(CHUNK_T):
                for c, r in enumerate(step_rows(k, i)):
                    y_scr[k, r, :] = y2[c * SUBLANES:(c + 1) * SUBLANES, i * LANES:(i + 1) * LANES]
        hs_scr[0:SUBLANES, :] = hs_scr[SUB_ROWS:SUB_ROWS + SUBLANES, :]
        return carry

    lax.fori_loop(0, SEQ_TILE // SUB_T, phase_b, 0, unroll=2)

    @pl.when(step == pl.num_programs(0) - 1)
    def _():
        for j in range(PAIRS):
            re, im, half = _pair_lanes(j)
            hre_ref[:, half] = hs_scr[0:SUBLANES, re]
            him_ref[:, half] = hs_scr[0:SUBLANES, im]

    def mixed_fn(v):
        vb = v.astype(BF16)
        rows = []
        for bb in range(SEQ_PER_CHUNK):
            heads = [jnp.dot(tril_scr[hd], vb[bb * CHUNK:(bb + 1) * CHUNK, hd * HEAD_DIM:(hd + 1) * HEAD_DIM],
                             preferred_element_type=F32) for hd in range(HEADS)]
            rows.append(jnp.concatenate(heads, axis=1) + bsp_ref[...])
        return jnp.concatenate(rows, axis=0)

    def phase_c(c, carry):
        xb, h = load_rows(c)
        gather = lambda scr: jnp.concatenate(
            [jnp.concatenate([scr[k, slab_rows(c, bb), :] for k in range(SLABS)], axis=1)
             for bb in range(SEQ_PER_CHUNK)], axis=0)
        gt = jnp.concatenate(
            [jnp.broadcast_to(gt_ref[pl.ds(c * SEQ_PER_CHUNK + bb, 1), :], (SEQ_TILE, D))
             for bb in range(SEQ_PER_CHUNK)], axis=0)
        xn, _ = _mix_back(xb, h, gather(ps_scr), gather(y_scr), mixed_fn, gt,
                          win_ref, gv_ref, dsk_ref, wglu_ref, bglu_ref, wout_ref)
        for bb in range(SEQ_PER_CHUNK):
            xo_ref[c * SEQ_PER_CHUNK + bb] = xn[bb * SEQ_TILE:(bb + 1) * SEQ_TILE]
        return carry

    lax.fori_loop(0, nb // SEQ_PER_CHUNK, phase_c, 0, unroll=2)


def _mix_prompt(x, mod, mod_row_block, l, p):
    nb, seq, _ = x.shape
    assert nb == SUBLANES and seq % SEQ_TILE == 0
    tile = pl.BlockSpec((nb, SEQ_TILE, D), lambda s: (0, s, 0))
    consts = [p['g1'], p['w_in'], p['g_v'], p['w_sp'], p['b_sp'], p['a_re'], p['a_im'],
              p['wb'], p['wc'], p['wt'], p['d_skip'], p['w_glu'], p['b_glu'], p['w_out']]
    state = jax.ShapeDtypeStruct((nb, S_HALF), F32)
    state_spec = pl.BlockSpec((nb, S_HALF), lambda s: (0, 0))
    return pl.pallas_call(
        _mix_prompt_kernel,
        grid=(seq // SEQ_TILE,),
        in_specs=[tile] + _mod_specs(l, nb, mod_row_block, (0, 1, 2)) + [_layer_spec(a, l) for a in consts],
        out_specs=(tile, state_spec, state_spec),
        out_shape=(jax.ShapeDtypeStruct(x.shape, F32), state, state),
        scratch_shapes=[
            pltpu.VMEM((SLABS, nb * PITCH, LANES), F32),
            pltpu.VMEM((SLABS, nb * PITCH, LANES), F32),
            pltpu.VMEM((SUB_ROWS + SUBLANES, S_W), F32),
            pltpu.VMEM((SLABS, SUB_ROWS, CHUNK_T * LANES), BF16),
            pltpu.VMEM((HEADS, CHUNK, CHUNK), BF16),
        ],
        compiler_params=pltpu.CompilerParams(
            dimension_semantics=("arbitrary",), vmem_limit_bytes=VMEM_LIMIT),
        name="mix_prompt",
    )(x, mod, mod, mod, *consts)


def _mix_sample_kernel(x_ref, sh_ref, sc_ref, gt_ref, g1_ref, win_ref, gv_ref, wd_ref, bd_ref,
                       ar_ref, ai_ref, wb_ref, wc_ref, wt_ref, dsk_ref, wglu_ref, bglu_ref, wout_ref,
                       h0r_ref, h0i_ref, xo_ref, hre_ref, him_ref, v_ref):
    xb = x_ref[...]
    h = (_rms(xb, g1_ref[...]) * (1.0 + sc_ref[...]) + sh_ref[...]).astype(BF16)
    ps = _bdot(h, win_ref[:, 2 * D:3 * D])
    ys = []
    for k in range(SLABS):
        xk = ps[:, k * LANES:(k + 1) * LANES].astype(BF16)
        bu = jnp.dot(xk, wb_ref[k, LANES:2 * LANES, :], preferred_element_type=F32)
        h0 = []
        for q in range(SLAB_PAIRS):
            re, im, half = _pair_lanes(q)
            _, _, half = _pair_lanes(k * SLAB_PAIRS + q)
            ar, ai = ar_ref[:, half], ai_ref[:, half]
            hr, hi = h0r_ref[:, half], h0i_ref[:, half]
            hre_ref[:, half] = ar * hr - ai * hi + bu[:, re]
            him_ref[:, half] = ar * hi + ai * hr + bu[:, im]
            h0 += [hr, hi]
        ys.append(_bdot(jnp.concatenate(h0, axis=1), wc_ref[k, :, 0:LANES])
                  + jnp.dot(xk, wt_ref[k, 0:LANES, 0:LANES], preferred_element_type=F32))
    y = jnp.concatenate(ys, axis=1)
    mixed_fn = lambda v: wd_ref[...] * v + bd_ref[...]
    xn, v = _mix_back(xb, h, ps, y, mixed_fn, gt_ref[...], win_ref, gv_ref, dsk_ref, wglu_ref, bglu_ref, wout_ref)
    xo_ref[...] = xn
    v_ref[...] = v


def _mix_sample(x, mod, l, h0_re, h0_im, p):
    rows = x.shape[0]
    consts = [p['g1'], p['w_in'], p['g_v'], p['w_diag'], p['b_diag'], p['a_re'], p['a_im'],
              p['wb'], p['wc'], p['wt'], p['d_skip'], p['w_glu'], p['b_glu'], p['w_out'], h0_re, h0_im]
    whole = lambda shape: pl.BlockSpec(shape, lambda i: (0,) * len(shape))
    state = jax.ShapeDtypeStruct((rows, S_HALF), F32)
    return pl.pallas_call(
        _mix_sample_kernel,
        grid=(1,),
        in_specs=[whole((rows, D))] + _mod_specs(l, rows, 0, (0, 1, 2)) + [_layer_spec(a, l) for a in consts],
        out_specs=(whole((rows, D)), whole((rows, S_HALF)), whole((rows, S_HALF)), whole((rows, D))),
        out_shape=(jax.ShapeDtypeStruct((rows, D), F32), state, state, jax.ShapeDtypeStruct((rows, D), F32)),
        compiler_params=pltpu.CompilerParams(
            dimension_semantics=("arbitrary",), vmem_limit_bytes=VMEM_LIMIT),
        name="mix_sample",
    )(x, mod, mod, mod, *consts)


def _mlp_kernel(x_ref, sh_ref, sc_ref, gt_ref, g2_ref, w1_ref, w2_ref, gf_ref, xo_ref, *, final, tiles_per_mod):
    x = x_ref[...]
    if tiles_per_mod:
        b = pl.program_id(0) // tiles_per_mod
        sh, sc, gt = (r[pl.ds(b, 1), :] for r in (sh_ref, sc_ref, gt_ref))
    else:
        sh, sc, gt = sh_ref[...], sc_ref[...], gt_ref[...]
    h = (_rms(x, g2_ref[...]) * (1.0 + sc) + sh).astype(BF16)
    acc = jnp.zeros(x.shape, F32)
    for k in range(D_FF // D):
        a = jnp.dot(h, w1_ref[:, k * D:(k + 1) * D], preferred_element_type=F32)
        acc = acc + _bdot(jnp.square(jnp.maximum(a, 0.0)), w2_ref[k * D:(k + 1) * D, :])
    xn = x + gt * acc
    if final:
        xn = _rms(xn, gf_ref[...])
    xo_ref[...] = xn


def _mlp(x, mod, mod_row_block, l, p, g_final, final):
    if x.ndim == 3:
        nb, seq, _ = x.shape
        tm = min(MLP_ROWS, seq)
        assert seq % tm == 0
        per = seq // tm
        grid = (nb * per,)
        tile = pl.BlockSpec((None, tm, D), lambda i: (i // per, i % per, 0))
        mod_rows = nb
    else:
        per = 0
        grid = (1,)
        tile = pl.BlockSpec(x.shape, lambda i: (0, 0))
        mod_rows = x.shape[0]
    return pl.pallas_call(
        functools.partial(_mlp_kernel, final=final, tiles_per_mod=per),
        grid=grid,
        in_specs=[tile] + _mod_specs(l, mod_rows, mod_row_block, (3, 4, 5)) + [
            _layer_spec(p['g2'], l), _layer_spec(p['w_ff1'], l), _layer_spec(p['w_ff2'], l),
            pl.BlockSpec((1, D), lambda i: (0, 0))],
        out_specs=tile,
        out_shape=jax.ShapeDtypeStruct(x.shape, F32),
        compiler_params=pltpu.CompilerParams(
            dimension_semantics=("arbitrary",), vmem_limit_bytes=VMEM_LIMIT),
        name="mlp_final" if final else "mlp",
    )(x, mod, mod, mod, p['g2'], p['w_ff1'], p['w_ff2'], g_final)


def kernel(x_prompt, x_sample, c_prompt, c_sample, state_ssm_re, state_ssm_im, w_ada, b_ada, g_norm1, g_norm2, w_in, g_v, w_spatial, b_spatial, lam_re, lam_im, log_dt, b_re, b_im, c_re, c_im, d_skip, w_glu, b_glu, w_out, w_ff1, w_ff2, g_final):
    nbp, seq, _ = x_prompt.shape
    nbs = x_sample.shape[0]
    assert x_sample.shape[1] == 1 and nbs % nbp == 0

    a_re, a_im, bb_re, bb_im = _s5_prep(lam_re, lam_im, log_dt, b_re, b_im)
    wb, wc, wt = _s5_chunk_maps(*_s5_slab_layout(a_re, a_im, bb_re, bb_im, c_re, c_im))
    mod = _ada(jnp.concatenate([c_sample, c_prompt], axis=0), w_ada, b_ada)
    prompt_row_block = nbs // nbp

    vec = lambda a: a.reshape(DEPTH, 1, D)
    p = dict(
        g1=vec(g_norm1), g2=vec(g_norm2), g_v=vec(g_v), d_skip=vec(d_skip), b_glu=vec(b_glu),
        w_in=w_in.astype(BF16), w_glu=w_glu.astype(BF16), w_out=w_out.astype(BF16),
        w_ff1=w_ff1.astype(BF16), w_ff2=w_ff2.astype(BF16),
        w_sp=w_spatial, b_sp=jnp.repeat(jnp.swapaxes(b_spatial, 1, 2), HEAD_DIM, axis=2),
        w_diag=vec(jnp.repeat(w_spatial[:, :, 0, 0], HEAD_DIM, axis=1)),
        b_diag=vec(jnp.repeat(b_spatial[:, :, 0], HEAD_DIM, axis=1)),
        a_re=a_re.reshape(DEPTH, 1, S_HALF), a_im=a_im.reshape(DEPTH, 1, S_HALF), wb=wb, wc=wc, wt=wt)
    gf = g_final.reshape(1, D)
    h0_re = state_ssm_re.reshape(DEPTH, nbs, S_HALF)
    h0_im = state_ssm_im.reshape(DEPTH, nbs, S_HALF)

    xp = x_prompt
    xs = x_sample.reshape(nbs, D)
    re_p, im_p, re_s, im_s, v_s = [], [], [], [], []
    for l in range(DEPTH):
        final = l == DEPTH - 1
        xp, hr, hi = _mix_prompt(xp, mod, prompt_row_block, l, p)
        xp = _mlp(xp, mod, prompt_row_block, l, p, gf, final)
        re_p.append(hr)
        im_p.append(hi)
        xs, hr, hi, v = _mix_sample(xs, mod, l, h0_re, h0_im, p)
        xs = _mlp(xs, mod, 0, l, p, gf, final)
        re_s.append(hr)
        im_s.append(hi)
        v_s.append(v)

    state = lambda hs, nb: jnp.stack(hs).reshape(DEPTH, nb, GROUPS, STATE)
    return (xp, xs.reshape(nbs, 1, D), state(re_p, nbp), state(im_p, nbp),
            state(re_s, nbs), state(im_s, nbs), jnp.stack(v_s).reshape(DEPTH, nbs, 1, D))
```

```python
import functools
import math

import jax
import jax.numpy as jnp
import numpy as np
from jax import lax
from jax.experimental import pallas as pl
from jax.experimental.pallas import tpu as pltpu

F32 = jnp.float32
BF16 = jnp.bfloat16

D = 1024
DEPTH = 2
CHUNK = 128
HEADS = 4
HEAD_DIM = D // HEADS
GROUPS = 64
GROUP_W = 16
STATE = 64
D_FF = 4 * D
N_IN = 5 * D
EPS = 1e-6

LANES = 128
SUBLANES = 8
PAIRS = GROUPS // 2
PAIR_W = 4 * STATE
S_W = PAIRS * PAIR_W
S_HALF = S_W // 2
SLABS = D // LANES
SLAB_PAIRS = LANES // (2 * GROUP_W)
SLAB_W = SLAB_PAIRS * PAIR_W
CHUNK_T = 2

SEQ_TILE = CHUNK
ROW_CHUNK = 256
SEQ_PER_CHUNK = ROW_CHUNK // SEQ_TILE
SUB_C = 16
SUB_ROWS = SUB_C * SUBLANES
SUB_T = SUB_C * CHUNK_T
PITCH = SEQ_TILE + SUBLANES
MLP_ROWS = 1024
VMEM_LIMIT = 61 * 1024 * 1024

_GELU_C = math.sqrt(2.0 / math.pi)


def _gelu(x):
    return x * (0.5 * (1.0 + jnp.tanh(_GELU_C * (x + 0.044715 * (x * x * x)))))


def _sigmoid(x):
    return 1.0 / (1.0 + jnp.exp(-x))


def _rms(x, g):
    return x * lax.rsqrt(jnp.mean(x * x, axis=-1, keepdims=True) + EPS) * g


def _bdot(a, b):
    return jnp.dot(a.astype(BF16), b, preferred_element_type=F32)


def _layer_spec(a, l):
    zeros = (0,) * (a.ndim - 1)
    return pl.BlockSpec((None,) + a.shape[1:], lambda *_: (l,) + zeros, pipeline_mode=pl.Buffered(1))


def _s5_prep_kernel(lr_ref, li_ref, ldt_ref, lrw_ref, liw_ref, ldtw_ref, br_ref, bi_ref,
                    ar_ref, ai_ref, bbr_ref, bbi_ref):
    def a_bar(lr, li, ldt):
        dt = jnp.exp(ldt)
        mag = jnp.exp(lr * dt)
        return mag * jnp.cos(li * dt), mag * jnp.sin(li * dt)

    ar, ai = a_bar(lr_ref[...], li_ref[...], ldt_ref[...])
    ar_ref[...] = ar
    ai_ref[...] = ai
    lr, li = lrw_ref[...], liw_ref[...]
    ar, ai = a_bar(lr, li, ldtw_ref[...])
    nr, ni = ar - 1.0, ai
    den = lr * lr + li * li
    cr = (nr * lr + ni * li) / den
    ci = (ni * lr - nr * li) / den
    br, bi = br_ref[...], bi_ref[...]
    bbr_ref[...] = cr * br - ci * bi
    bbi_ref[...] = cr * bi + ci * br


def _s5_prep(lam_re, lam_im, log_dt, b_re, b_im):
    wide = (DEPTH, GROUPS, STATE * GROUP_W)
    ldt = jnp.broadcast_to(log_dt[:, :, None], lam_re.shape)
    lam_w = jnp.repeat(jnp.stack([lam_re, lam_im, ldt]), GROUP_W, axis=-1)
    small = jax.ShapeDtypeStruct(lam_re.shape, F32)
    big = jax.ShapeDtypeStruct(wide, F32)
    return pl.pallas_call(
        _s5_prep_kernel, out_shape=(small, small, big, big), name="s5_prep",
    )(lam_re, lam_im, ldt, lam_w[0], lam_w[1], lam_w[2], b_re.reshape(wide), b_im.reshape(wide))


def _s5_slab_layout(a_re, a_im, bb_re, bb_im, c_re, c_im):
    bb = jnp.stack([bb_re, bb_im]).reshape(2, DEPTH, SLABS, SLAB_PAIRS, 2, STATE, GROUP_W)
    b = jnp.transpose(bb, (1, 2, 3, 4, 6, 0, 5)).reshape(DEPTH, SLABS, LANES, 2 * STATE)
    cc = jnp.stack([c_re, -c_im]).reshape(2, DEPTH, SLABS, LANES, STATE)
    ct = jnp.transpose(cc, (1, 2, 3, 0, 4)).reshape(DEPTH, SLABS, LANES, 2 * STATE)

    def lanes(a):
        a = a.reshape(DEPTH, SLABS, SLAB_PAIRS, 1, 2 * STATE)
        return jnp.broadcast_to(a, (DEPTH, SLABS, SLAB_PAIRS, 2, 2 * STATE)).reshape(DEPTH, SLABS, 1, SLAB_W)

    return b, ct, lanes(a_re), lanes(a_im)


def _spread_matrix():
    spread = np.zeros((2, STATE, SLAB_PAIRS, 2, 2, STATE), np.float32)
    for r in range(2):
        spread[r, np.arange(STATE), :, r, :, np.arange(STATE)] = 1.0
    return jnp.asarray(spread.reshape(2 * STATE, SLAB_W), BF16)


def _cmul_cols(m, ar_row, ai_row, sign):
    parts = []
    for q in range(SLAB_PAIRS):
        re = slice(q * PAIR_W, q * PAIR_W + LANES)
        im = slice(q * PAIR_W + LANES, (q + 1) * PAIR_W)
        ar, ai = ar_row[:, re], sign * ai_row[:, re]
        parts += [m[:, re] * ar - m[:, im] * ai, m[:, im] * ar + m[:, re] * ai]
    return jnp.concatenate(parts, axis=1)


def _s5_chunk_kernel(b_ref, ct_ref, spread_ref, ar_ref, ai_ref, wb_ref, wc_ref, wt_ref):
    lane = lax.broadcasted_iota(jnp.int32, (LANES, SLAB_W), 1)
    chan = lax.broadcasted_iota(jnp.int32, (LANES, SLAB_W), 0)
    same_group = (lane // PAIR_W) * 2 + (lane // STATE) % 2 == chan // GROUP_W
    b = jnp.where(same_group, _bdot(b_ref[...], spread_ref[...]), 0.0)
    ct = jnp.where(same_group, _bdot(ct_ref[...], spread_ref[...]), 0.0)
    ar, ai = ar_ref[...], ai_ref[...]
    ab = _cmul_cols(b, ar, ai, 1.0)
    wb_ref[0:LANES, :] = ab.astype(BF16)
    wb_ref[LANES:2 * LANES, :] = b.astype(BF16)
    cat = _cmul_cols(ct, ar, ai, -1.0)
    wc_ref[:, 0:LANES] = cat.T.astype(BF16)
    wc_ref[:, LANES:2 * LANES] = _cmul_cols(cat, ar, ai, -1.0).T.astype(BF16)
    ctb = ct.astype(BF16)
    times_c = lambda m: lax.dot_general(m.astype(BF16), ctb, (((1,), (1,)), ((), ())),
                                        preferred_element_type=F32).astype(BF16)
    k0 = times_c(b)
    wt_ref[0:LANES, 0:LANES] = k0
    wt_ref[0:LANES, LANES:2 * LANES] = times_c(ab)
    wt_ref[LANES:2 * LANES, 0:LANES] = jnp.zeros((LANES, LANES), BF16)
    wt_ref[LANES:2 * LANES, LANES:2 * LANES] = k0


def _s5_chunk_maps(b, ct, a_re_lanes, a_im_lanes):
    spread = _spread_matrix()
    blk = lambda r, c: pl.BlockSpec((None, None, r, c), lambda l, k: (l, k, 0, 0))
    out = lambda r, c: jax.ShapeDtypeStruct((DEPTH, SLABS, r, c), BF16)
    return pl.pallas_call(
        _s5_chunk_kernel,
        grid=(DEPTH, SLABS),
        in_specs=[blk(LANES, 2 * STATE), blk(LANES, 2 * STATE), pl.BlockSpec(spread.shape, lambda l, k: (0, 0)),
                  blk(1, SLAB_W), blk(1, SLAB_W)],
        out_specs=(blk(2 * LANES, SLAB_W), blk(SLAB_W, 2 * LANES), blk(2 * LANES, 2 * LANES)),
        out_shape=(out(2 * LANES, SLAB_W), out(SLAB_W, 2 * LANES), out(2 * LANES, 2 * LANES)),
        name="s5_chunk_maps",
    )(b, ct, spread, a_re_lanes, a_im_lanes)


ADA_TILE = 1536


def _ada_kernel(c_ref, w_ref, b_ref, o_ref):
    c = c_ref[...]
    o_ref[...] = _bdot(c * _sigmoid(c), w_ref[...].astype(BF16)) + b_ref[...]


def _ada(c_all, w_ada, b_ada):
    rows = c_all.shape[0]
    return pl.pallas_call(
        _ada_kernel,
        grid=(DEPTH, 6 * D // ADA_TILE),
        in_specs=[
            pl.BlockSpec((rows, D), lambda l, n: (0, 0)),
            pl.BlockSpec((None, D, ADA_TILE), lambda l, n: (l, 0, n)),
            pl.BlockSpec((None, 1, ADA_TILE), lambda l, n: (l, 0, n)),
        ],
        out_specs=pl.BlockSpec((None, rows, ADA_TILE), lambda l, n: (l, 0, n)),
        out_shape=jax.ShapeDtypeStruct((DEPTH, rows, 6 * D), F32),
        name="adaln",
    )(c_all, w_ada, b_ada.reshape(DEPTH, 1, 6 * D))


def _mod_specs(l, rows, row_block, which):
    return [pl.BlockSpec((None, rows, D), lambda *_, n=n: (l, row_block, n)) for n in which]


def _mix_back(xb, h, ps, y, mixed_fn, gt, win_ref, gv_ref, dsk_ref, wglu_ref, bglu_ref, wout_ref):
    v = _rms(_gelu(_bdot(h, win_ref[:, D:2 * D])), gv_ref[...])
    y_a = _gelu(_bdot(h, win_ref[:, 0:D])) * mixed_fn(v)
    merged = _sigmoid(_bdot(h, win_ref[:, 3 * D:4 * D])) * y_a
    z = _gelu(y + dsk_ref[...] * ps)
    y_b = z * _sigmoid(_bdot(z, wglu_ref[...]) + bglu_ref[...])
    merged = merged + _sigmoid(_bdot(h, win_ref[:, 4 * D:5 * D])) * y_b
    return xb + gt * _bdot(merged, wout_ref[...]), v


def _pair_lanes(j):
    re = slice(j * PAIR_W, j * PAIR_W + LANES)
    im = slice(j * PAIR_W + LANES, (j + 1) * PAIR_W)
    half = slice(j * LANES, (j + 1) * LANES)
    return re, im, half


def _mix_prompt_kernel(x_ref, sh_ref, sc_ref, gt_ref, g1_ref, win_ref, gv_ref, wsp_ref, bsp_ref,
                       ar_ref, ai_ref, wb_ref, wc_ref, wt_ref, dsk_ref, wglu_ref, bglu_ref, wout_ref,
                       xo_ref, hre_ref, him_ref, ps_scr, y_scr, hs_scr, xc_scr, tril_scr):
    nb = x_ref.shape[0]
    assert nb == SUBLANES
    step = pl.program_id(0)

    @pl.when(step == 0)
    def _():
        hs_scr[0:SUBLANES, :] = jnp.zeros((SUBLANES, S_W), F32)

    row = lax.broadcasted_iota(jnp.int32, (CHUNK, CHUNK), 0)
    col = lax.broadcasted_iota(jnp.int32, (CHUNK, CHUNK), 1)
    for hd in range(HEADS):
        tril_scr[hd] = jnp.where(row >= col, wsp_ref[hd], 0.0).astype(BF16)

    def load_rows(c):
        xs, hs = [], []
        for bb in range(SEQ_PER_CHUNK):
            b = c * SEQ_PER_CHUNK + bb
            x = x_ref[b]
            xs.append(x)
            hs.append(_rms(x, g1_ref[...]) * (1.0 + sc_ref[pl.ds(b, 1), :]) + sh_ref[pl.ds(b, 1), :])
        return jnp.concatenate(xs, axis=0), jnp.concatenate(hs, axis=0).astype(BF16)

    def slab_rows(c, bb):
        return pl.ds(pl.multiple_of((c * SEQ_PER_CHUNK + bb) * PITCH, SUBLANES), SEQ_TILE)

    def phase_a(c, carry):
        _, h = load_rows(c)
        ps = _bdot(h, win_ref[:, 2 * D:3 * D])
        for bb in range(SEQ_PER_CHUNK):
            for k in range(SLABS):
                ps_scr[k, slab_rows(c, bb), :] = ps[bb * SEQ_TILE:(bb + 1) * SEQ_TILE, k * LANES:(k + 1) * LANES]
        return carry

    lax.fori_loop(0, nb // SEQ_PER_CHUNK, phase_a, 0, unroll=True)

    def phase_b(sub, carry):
        t0 = sub * SUB_T

        def step_rows(k, i):
            return [pl.ds(t0 + CHUNK_T * c + i, SUBLANES, stride=PITCH) for c in range(SUB_C)]

        for k in range(SLABS):
            xc = jnp.concatenate(
                [jnp.concatenate([ps_scr[k, r, :] for r in step_rows(k, i)], axis=0) for i in range(CHUNK_T)],
                axis=1).astype(BF16)
            xc_scr[k] = xc
            hs_scr[SUBLANES:SUBLANES + SUB_ROWS, k * SLAB_W:(k + 1) * SLAB_W] = jnp.dot(
                xc, wb_ref[k], preferred_element_type=F32)
        for j in range(PAIRS):
            re, im, half = _pair_lanes(j)
            ar = jnp.broadcast_to(ar_ref[:, half], (SUBLANES, LANES))
            ai = jnp.broadcast_to(ai_ref[:, half], (SUBLANES, LANES))
            ar, ai = ar * ar - ai * ai, 2.0 * (ar * ai)
            hr = hs_scr[0:SUBLANES, re]
            hi = hs_scr[0:SUBLANES, im]
            for c in range(SUB_C):
                r = slice(SUBLANES * (c + 1), SUBLANES * (c + 2))
                hr, hi = (ar * hr - ai * hi + hs_scr[r, re], ar * hi + ai * hr + hs_scr[r, im])
                hs_scr[r, re] = hr
                hs_scr[r, im] = hi
        for k in range(SLABS):
            h_prev = hs_scr[0:SUB_ROWS, k * SLAB_W:(k + 1) * SLAB_W]
            y2 = (jnp.dot(xc_scr[k], wt_ref[k], preferred_element_type=F32)
                  + jnp.dot(h_prev.astype(BF16), wc_ref[k], preferred_element_type=F32))
            for i in range(CHUNK_T):
                for c, r in enumerate(step_rows(k, i)):
                    y_scr[k, r, :] = y2[c * SUBLANES:(c + 1) * SUBLANES, i * LANES:(i + 1) * LANES]
        hs_scr[0:SUBLANES, :] = hs_scr[SUB_ROWS:SUB_ROWS + SUBLANES, :]
        return carry

    lax.fori_loop(0, SEQ_TILE // SUB_T, phase_b, 0, unroll=2)

    @pl.when(step == pl.num_programs(0) - 1)
    def _():
        for j in range(PAIRS):
            re, im, half = _pair_lanes(j)
            hre_ref[:, half] = hs_scr[0:SUBLANES, re]
            him_ref[:, half] = hs_scr[0:SUBLANES, im]

    def mixed_fn(v):
        vb = v.astype(BF16)
        rows = []
        for bb in range(SEQ_PER_CHUNK):
            heads = [jnp.dot(tril_scr[hd], vb[bb * CHUNK:(bb + 1) * CHUNK, hd * HEAD_DIM:(hd + 1) * HEAD_DIM],
                             preferred_element_type=F32) for hd in range(HEADS)]
            rows.append(jnp.concatenate(heads, axis=1) + bsp_ref[...])
        return jnp.concatenate(rows, axis=0)

    def phase_c(c, carry):
        xb, h = load_rows(c)
        gather = lambda scr: jnp.concatenate(
            [jnp.concatenate([scr[k, slab_rows(c, bb), :] for k in range(SLABS)], axis=1)
             for bb in range(SEQ_PER_CHUNK)], axis=0)
        gt = jnp.concatenate(
            [jnp.broadcast_to(gt_ref[pl.ds(c * SEQ_PER_CHUNK + bb, 1), :], (SEQ_TILE, D))
             for bb in range(SEQ_PER_CHUNK)], axis=0)
        xn, _ = _mix_back(xb, h, gather(ps_scr), gather(y_scr), mixed_fn, gt,
                          win_ref, gv_ref, dsk_ref, wglu_ref, bglu_ref, wout_ref)
        for bb in range(SEQ_PER_CHUNK):
            xo_ref[c * SEQ_PER_CHUNK + bb] = xn[bb * SEQ_TILE:(bb + 1) * SEQ_TILE]
        return carry

    lax.fori_loop(0, nb // SEQ_PER_CHUNK, phase_c, 0, unroll=2)


def _mix_prompt(x, mod, mod_row_block, l, p):
    nb, seq, _ = x.shape
    assert nb == SUBLANES and seq % SEQ_TILE == 0
    tile = pl.BlockSpec((nb, SEQ_TILE, D), lambda s: (0, s, 0))
    consts = [p['g1'], p['w_in'], p['g_v'], p['w_sp'], p['b_sp'], p['a_re'], p['a_im'],
              p['wb'], p['wc'], p['wt'], p['d_skip'], p['w_glu'], p['b_glu'], p['w_out']]
    state = jax.ShapeDtypeStruct((nb, S_HALF), F32)
    state_spec = pl.BlockSpec((nb, S_HALF), lambda s: (0, 0))
    return pl.pallas_call(
        _mix_prompt_kernel,
        grid=(seq // SEQ_TILE,),
        in_specs=[tile] + _mod_specs(l, nb, mod_row_block, (0, 1, 2)) + [_layer_spec(a, l) for a in consts],
        out_specs=(tile, state_spec, state_spec),
        out_shape=(jax.ShapeDtypeStruct(x.shape, F32), state, state),
        scratch_shapes=[
            pltpu.VMEM((SLABS, nb * PITCH, LANES), F32),
            pltpu.VMEM((SLABS, nb * PITCH, LANES), F32),
            pltpu.VMEM((SUB_ROWS + SUBLANES, S_W), F32),
            pltpu.VMEM((SLABS, SUB_ROWS, CHUNK_T * LANES), BF16),
            pltpu.VMEM((HEADS, CHUNK, CHUNK), BF16),
        ],
        compiler_params=pltpu.CompilerParams(
            dimension_semantics=("arbitrary",), vmem_limit_bytes=VMEM_LIMIT),
        name="mix_prompt",
    )(x, mod, mod, mod, *consts)


def _mix_sample_kernel(x_ref, sh_ref, sc_ref, gt_ref, g1_ref, win_ref, gv_ref, wd_ref, bd_ref,
                       ar_ref, ai_ref, wb_ref, wc_ref, wt_ref, dsk_ref, wglu_ref, bglu_ref, wout_ref,
                       h0r_ref, h0i_ref, xo_ref, hre_ref, him_ref, v_ref):
    xb = x_ref[...]
    h = (_rms(xb, g1_ref[...]) * (1.0 + sc_ref[...]) + sh_ref[...]).astype(BF16)
    ps = _bdot(h, win_ref[:, 2 * D:3 * D])
    ys = []
    for k in range(SLABS):
        xk = ps[:, k * LANES:(k + 1) * LANES].astype(BF16)
        bu = jnp.dot(xk, wb_ref[k, LANES:2 * LANES, :], preferred_element_type=F32)
        h0 = []
        for q in range(SLAB_PAIRS):
            re, im, half = _pair_lanes(q)
            _, _, half = _pair_lanes(k * SLAB_PAIRS + q)
            ar, ai = ar_ref[:, half], ai_ref[:, half]
            hr, hi = h0r_ref[half, :].T, h0i_ref[half, :].T
            hre_ref[half, :] = (ar * hr - ai * hi + bu[:, re]).T
            him_ref[half, :] = (ar * hi + ai * hr + bu[:, im]).T
            h0 += [hr, hi]
        ys.append(_bdot(jnp.concatenate(h0, axis=1), wc_ref[k, :, 0:LANES])
                  + jnp.dot(xk, wt_ref[k, 0:LANES, 0:LANES], preferred_element_type=F32))
    y = jnp.concatenate(ys, axis=1)
    mixed_fn = lambda v: wd_ref[...] * v + bd_ref[...]
    xn, v = _mix_back(xb, h, ps, y, mixed_fn, gt_ref[...], win_ref, gv_ref, dsk_ref, wglu_ref, bglu_ref, wout_ref)
    xo_ref[...] = xn
    v_ref[...] = v


def _mix_sample(x, mod, l, h0_re, h0_im, p):
    rows = x.shape[0]
    assert rows == LANES
    consts = [p['g1'], p['w_in'], p['g_v'], p['w_diag'], p['b_diag'], p['a_re'], p['a_im'],
              p['wb'], p['wc'], p['wt'], p['d_skip'], p['w_glu'], p['b_glu'], p['w_out'], h0_re, h0_im]
    whole = lambda shape: pl.BlockSpec(shape, lambda i: (0,) * len(shape))
    state = jax.ShapeDtypeStruct((S_HALF, rows), F32)
    return pl.pallas_call(
        _mix_sample_kernel,
        grid=(1,),
        in_specs=[whole((rows, D))] + _mod_specs(l, rows, 0, (0, 1, 2)) + [_layer_spec(a, l) for a in consts],
        out_specs=(whole((rows, D)), whole((S_HALF, rows)), whole((S_HALF, rows)), whole((rows, D))),
        out_shape=(jax.ShapeDtypeStruct((rows, D), F32), state, state, jax.ShapeDtypeStruct((rows, D), F32)),
        compiler_params=pltpu.CompilerParams(
            dimension_semantics=("arbitrary",), vmem_limit_bytes=VMEM_LIMIT),
        name="mix_sample",
    )(x, mod, mod, mod, *consts)


def _mlp_kernel(x_ref, sh_ref, sc_ref, gt_ref, g2_ref, w1_ref, w2_ref, gf_ref, xo_ref, *, final, tiles_per_mod):
    x = x_ref[...]
    if tiles_per_mod:
        b = pl.program_id(0) // tiles_per_mod
        sh, sc, gt = (r[pl.ds(b, 1), :] for r in (sh_ref, sc_ref, gt_ref))
    else:
        sh, sc, gt = sh_ref[...], sc_ref[...], gt_ref[...]
    h = (_rms(x, g2_ref[...]) * (1.0 + sc) + sh).astype(BF16)
    acc = jnp.zeros(x.shape, F32)
    for k in range(D_FF // D):
        a = jnp.dot(h, w1_ref[:, k * D:(k + 1) * D], preferred_element_type=F32)
        acc = acc + _bdot(jnp.square(jnp.maximum(a, 0.0)), w2_ref[k * D:(k + 1) * D, :])
    xn = x + gt * acc
    if final:
        xn = _rms(xn, gf_ref[...])
    xo_ref[...] = xn


def _mlp(x, mod, mod_row_block, l, p, g_final, final):
    if x.ndim == 3:
        nb, seq, _ = x.shape
        tm = min(MLP_ROWS, seq)
        assert seq % tm == 0
        per = seq // tm
        grid = (nb * per,)
        tile = pl.BlockSpec((None, tm, D), lambda i: (i // per, i % per, 0))
        mod_rows = nb
    else:
        per = 0
        grid = (1,)
        tile = pl.BlockSpec(x.shape, lambda i: (0, 0))
        mod_rows = x.shape[0]
    return pl.pallas_call(
        functools.partial(_mlp_kernel, final=final, tiles_per_mod=per),
        grid=grid,
        in_specs=[tile] + _mod_specs(l, mod_rows, mod_row_block, (3, 4, 5)) + [
            _layer_spec(p['g2'], l), _layer_spec(p['w_ff1'], l), _layer_spec(p['w_ff2'], l),
            pl.BlockSpec((1, D), lambda i: (0, 0))],
        out_specs=tile,
        out_shape=jax.ShapeDtypeStruct(x.shape, F32),
        compiler_params=pltpu.CompilerParams(
            dimension_semantics=("arbitrary",), vmem_limit_bytes=VMEM_LIMIT),
        name="mlp_final" if final else "mlp",
    )(x, mod, mod, mod, p['g2'], p['w_ff1'], p['w_ff2'], g_final)


def kernel(x_prompt, x_sample, c_prompt, c_sample, state_ssm_re, state_ssm_im, w_ada, b_ada, g_norm1, g_norm2, w_in, g_v, w_spatial, b_spatial, lam_re, lam_im, log_dt, b_re, b_im, c_re, c_im, d_skip, w_glu, b_glu, w_out, w_ff1, w_ff2, g_final):
    nbp, seq, _ = x_prompt.shape
    nbs = x_sample.shape[0]
    assert x_sample.shape[1] == 1 and nbs % nbp == 0

    a_re, a_im, bb_re, bb_im = _s5_prep(lam_re, lam_im, log_dt, b_re, b_im)
    wb, wc, wt = _s5_chunk_maps(*_s5_slab_layout(a_re, a_im, bb_re, bb_im, c_re, c_im))
    mod = _ada(jnp.concatenate([c_sample, c_prompt], axis=0), w_ada, b_ada)
    prompt_row_block = nbs // nbp

    vec = lambda a: a.reshape(DEPTH, 1, D)
    p = dict(
        g1=vec(g_norm1), g2=vec(g_norm2), g_v=vec(g_v), d_skip=vec(d_skip), b_glu=vec(b_glu),
        w_in=w_in.astype(BF16), w_glu=w_glu.astype(BF16), w_out=w_out.astype(BF16),
        w_ff1=w_ff1.astype(BF16), w_ff2=w_ff2.astype(BF16),
        w_sp=w_spatial, b_sp=jnp.repeat(jnp.swapaxes(b_spatial, 1, 2), HEAD_DIM, axis=2),
        w_diag=vec(jnp.repeat(w_spatial[:, :, 0, 0], HEAD_DIM, axis=1)),
        b_diag=vec(jnp.repeat(b_spatial[:, :, 0], HEAD_DIM, axis=1)),
        a_re=a_re.reshape(DEPTH, 1, S_HALF), a_im=a_im.reshape(DEPTH, 1, S_HALF), wb=wb, wc=wc, wt=wt)
    gf = g_final.reshape(1, D)
    lane_major = lambda s: jnp.transpose(s, (0, 2, 3, 1)).reshape(DEPTH, S_HALF, nbs)
    h0_re, h0_im = lane_major(state_ssm_re), lane_major(state_ssm_im)

    xp = x_prompt
    xs = x_sample.reshape(nbs, D)
    re_p, im_p, re_s, im_s, v_s = [], [], [], [], []
    for l in range(DEPTH):
        final = l == DEPTH - 1
        xp, hr, hi = _mix_prompt(xp, mod, prompt_row_block, l, p)
        xp = _mlp(xp, mod, prompt_row_block, l, p, gf, final)
        re_p.append(hr)
        im_p.append(hi)
        xs, hr, hi, v = _mix_sample(xs, mod, l, h0_re, h0_im, p)
        xs = _mlp(xs, mod, 0, l, p, gf, final)
        re_s.append(hr)
        im_s.append(hi)
        v_s.append(v)

    state = lambda hs: jnp.stack(hs).reshape(DEPTH, nbp, GROUPS, STATE)
    state_t = lambda hs: jnp.transpose(jnp.stack(hs).reshape(DEPTH, GROUPS, STATE, nbs), (0, 3, 1, 2))
    return (xp, xs.reshape(nbs, 1, D), state(re_p), state(im_p),
            state_t(re_s), state_t(im_s), jnp.stack(v_s).reshape(DEPTH, nbs, 1, D))
```

```python
import functools
import math

import jax
import jax.numpy as jnp
import numpy as np
from jax import lax
from jax.experimental import pallas as pl
from jax.experimental.pallas import tpu as pltpu

F32 = jnp.float32
BF16 = jnp.bfloat16

D = 1024
DEPTH = 2
CHUNK = 128
HEADS = 4
HEAD_DIM = D // HEADS
GROUPS = 64
GROUP_W = 16
STATE = 64
D_FF = 4 * D
N_IN = 5 * D
EPS = 1e-6

LANES = 128
SUBLANES = 8
PAIRS = GROUPS // 2
PAIR_W = 4 * STATE
S_W = PAIRS * PAIR_W
S_HALF = S_W // 2
SLABS = D // LANES
SLAB_PAIRS = LANES // (2 * GROUP_W)
SLAB_W = SLAB_PAIRS * PAIR_W
CHUNK_T = 2

SEQ_TILE = CHUNK
ROW_CHUNK = 256
SEQ_PER_CHUNK = ROW_CHUNK // SEQ_TILE
SUB_C = 32
SUB_ROWS = SUB_C * SUBLANES
SUB_T = SUB_C * CHUNK_T
PITCH = SEQ_TILE + SUBLANES
MLP_ROWS = 1024
VMEM_LIMIT = 61 * 1024 * 1024

_GELU_C = math.sqrt(2.0 / math.pi)


def _gelu(x):
    return x * (0.5 * (1.0 + jnp.tanh(_GELU_C * (x + 0.044715 * (x * x * x)))))


def _sigmoid(x):
    return 1.0 / (1.0 + jnp.exp(-x))


def _rms(x, g):
    return x * lax.rsqrt(jnp.mean(x * x, axis=-1, keepdims=True) + EPS) * g


def _bdot(a, b):
    return jnp.dot(a.astype(BF16), b, preferred_element_type=F32)


def _layer_spec(a, l):
    zeros = (0,) * (a.ndim - 1)
    return pl.BlockSpec((None,) + a.shape[1:], lambda *_: (l,) + zeros, pipeline_mode=pl.Buffered(1))


def _s5_prep_kernel(lr_ref, li_ref, ldt_ref, lrw_ref, liw_ref, ldtw_ref, br_ref, bi_ref,
                    ar_ref, ai_ref, bbr_ref, bbi_ref):
    def a_bar(lr, li, ldt):
        dt = jnp.exp(ldt)
        mag = jnp.exp(lr * dt)
        return mag * jnp.cos(li * dt), mag * jnp.sin(li * dt)

    ar, ai = a_bar(lr_ref[...], li_ref[...], ldt_ref[...])
    ar_ref[...] = ar
    ai_ref[...] = ai
    lr, li = lrw_ref[...], liw_ref[...]
    ar, ai = a_bar(lr, li, ldtw_ref[...])
    nr, ni = ar - 1.0, ai
    den = lr * lr + li * li
    cr = (nr * lr + ni * li) / den
    ci = (ni * lr - nr * li) / den
    br, bi = br_ref[...], bi_ref[...]
    bbr_ref[...] = cr * br - ci * bi
    bbi_ref[...] = cr * bi + ci * br


def _s5_prep(lam_re, lam_im, log_dt, b_re, b_im):
    wide = (DEPTH, GROUPS, STATE * GROUP_W)
    ldt = jnp.broadcast_to(log_dt[:, :, None], lam_re.shape)
    lam_w = jnp.repeat(jnp.stack([lam_re, lam_im, ldt]), GROUP_W, axis=-1)
    small = jax.ShapeDtypeStruct(lam_re.shape, F32)
    big = jax.ShapeDtypeStruct(wide, F32)
    return pl.pallas_call(
        _s5_prep_kernel, out_shape=(small, small, big, big), name="s5_prep",
    )(lam_re, lam_im, ldt, lam_w[0], lam_w[1], lam_w[2], b_re.reshape(wide), b_im.reshape(wide))


def _s5_slab_layout(a_re, a_im, bb_re, bb_im, c_re, c_im):
    bb = jnp.stack([bb_re, bb_im]).reshape(2, DEPTH, SLABS, SLAB_PAIRS, 2, STATE, GROUP_W)
    b = jnp.transpose(bb, (1, 2, 3, 4, 6, 0, 5)).reshape(DEPTH, SLABS, LANES, 2 * STATE)
    cc = jnp.stack([c_re, -c_im]).reshape(2, DEPTH, SLABS, LANES, STATE)
    ct = jnp.transpose(cc, (1, 2, 3, 0, 4)).reshape(DEPTH, SLABS, LANES, 2 * STATE)

    def lanes(a):
        a = a.reshape(DEPTH, SLABS, SLAB_PAIRS, 1, 2 * STATE)
        return jnp.broadcast_to(a, (DEPTH, SLABS, SLAB_PAIRS, 2, 2 * STATE)).reshape(DEPTH, SLABS, 1, SLAB_W)

    return b, ct, lanes(a_re), lanes(a_im)


def _spread_matrix():
    spread = np.zeros((2, STATE, SLAB_PAIRS, 2, 2, STATE), np.float32)
    for r in range(2):
        spread[r, np.arange(STATE), :, r, :, np.arange(STATE)] = 1.0
    return jnp.asarray(spread.reshape(2 * STATE, SLAB_W), BF16)


def _cmul_cols(m, ar_row, ai_row, sign):
    parts = []
    for q in range(SLAB_PAIRS):
        re = slice(q * PAIR_W, q * PAIR_W + LANES)
        im = slice(q * PAIR_W + LANES, (q + 1) * PAIR_W)
        ar, ai = ar_row[:, re], sign * ai_row[:, re]
        parts += [m[:, re] * ar - m[:, im] * ai, m[:, im] * ar + m[:, re] * ai]
    return jnp.concatenate(parts, axis=1)


def _s5_chunk_kernel(b_ref, ct_ref, spread_ref, ar_ref, ai_ref, wb_ref, wc_ref, wt_ref):
    lane = lax.broadcasted_iota(jnp.int32, (LANES, SLAB_W), 1)
    chan = lax.broadcasted_iota(jnp.int32, (LANES, SLAB_W), 0)
    same_group = (lane // PAIR_W) * 2 + (lane // STATE) % 2 == chan // GROUP_W
    b = jnp.where(same_group, _bdot(b_ref[...], spread_ref[...]), 0.0)
    ct = jnp.where(same_group, _bdot(ct_ref[...], spread_ref[...]), 0.0)
    ar, ai = ar_ref[...], ai_ref[...]
    ab = _cmul_cols(b, ar, ai, 1.0)
    wb_ref[0:LANES, :] = ab.astype(BF16)
    wb_ref[LANES:2 * LANES, :] = b.astype(BF16)
    cat = _cmul_cols(ct, ar, ai, -1.0)
    wc_ref[:, 0:LANES] = cat.T.astype(BF16)
    wc_ref[:, LANES:2 * LANES] = _cmul_cols(cat, ar, ai, -1.0).T.astype(BF16)
    ctb = ct.astype(BF16)
    times_c = lambda m: lax.dot_general(m.astype(BF16), ctb, (((1,), (1,)), ((), ())),
                                        preferred_element_type=F32).astype(BF16)
    k0 = times_c(b)
    wt_ref[0:LANES, 0:LANES] = k0
    wt_ref[0:LANES, LANES:2 * LANES] = times_c(ab)
    wt_ref[LANES:2 * LANES, 0:LANES] = jnp.zeros((LANES, LANES), BF16)
    wt_ref[LANES:2 * LANES, LANES:2 * LANES] = k0


def _s5_chunk_maps(b, ct, a_re_lanes, a_im_lanes):
    spread = _spread_matrix()
    blk = lambda r, c: pl.BlockSpec((None, None, r, c), lambda l, k: (l, k, 0, 0))
    out = lambda r, c: jax.ShapeDtypeStruct((DEPTH, SLABS, r, c), BF16)
    return pl.pallas_call(
        _s5_chunk_kernel,
        grid=(DEPTH, SLABS),
        in_specs=[blk(LANES, 2 * STATE), blk(LANES, 2 * STATE), pl.BlockSpec(spread.shape, lambda l, k: (0, 0)),
                  blk(1, SLAB_W), blk(1, SLAB_W)],
        out_specs=(blk(2 * LANES, SLAB_W), blk(SLAB_W, 2 * LANES), blk(2 * LANES, 2 * LANES)),
        out_shape=(out(2 * LANES, SLAB_W), out(SLAB_W, 2 * LANES), out(2 * LANES, 2 * LANES)),
        name="s5_chunk_maps",
    )(b, ct, spread, a_re_lanes, a_im_lanes)


ADA_TILE = 1536


def _ada_kernel(c_ref, w_ref, b_ref, o_ref):
    c = c_ref[...]
    o_ref[...] = _bdot(c * _sigmoid(c), w_ref[...].astype(BF16)) + b_ref[...]


def _ada(c_all, w_ada, b_ada):
    rows = c_all.shape[0]
    return pl.pallas_call(
        _ada_kernel,
        grid=(DEPTH, 6 * D // ADA_TILE),
        in_specs=[
            pl.BlockSpec((rows, D), lambda l, n: (0, 0)),
            pl.BlockSpec((None, D, ADA_TILE), lambda l, n: (l, 0, n)),
            pl.BlockSpec((None, 1, ADA_TILE), lambda l, n: (l, 0, n)),
        ],
        out_specs=pl.BlockSpec((None, rows, ADA_TILE), lambda l, n: (l, 0, n)),
        out_shape=jax.ShapeDtypeStruct((DEPTH, rows, 6 * D), F32),
        name="adaln",
    )(c_all, w_ada, b_ada.reshape(DEPTH, 1, 6 * D))


def _mod_specs(l, rows, row_block, which):
    return [pl.BlockSpec((None, rows, D), lambda *_, n=n: (l, row_block, n)) for n in which]


def _mix_back(xb, h, y, mixed_fn, gt, win_ref, gv_ref, wglu_ref, bglu_ref, wout_ref):
    v = _rms(_gelu(_bdot(h, win_ref[:, D:2 * D])), gv_ref[...])
    y_a = _gelu(_bdot(h, win_ref[:, 0:D])) * mixed_fn(v)
    merged = _sigmoid(_bdot(h, win_ref[:, 3 * D:4 * D])) * y_a
    z = _gelu(y)
    y_b = z * _sigmoid(_bdot(z, wglu_ref[...]) + bglu_ref[...])
    merged = merged + _sigmoid(_bdot(h, win_ref[:, 4 * D:5 * D])) * y_b
    return xb + gt * _bdot(merged, wout_ref[...]), v


def _pair_lanes(j):
    re = slice(j * PAIR_W, j * PAIR_W + LANES)
    im = slice(j * PAIR_W + LANES, (j + 1) * PAIR_W)
    half = slice(j * LANES, (j + 1) * LANES)
    return re, im, half


def _mix_prompt_kernel(x_ref, sh_ref, sc_ref, gt_ref, g1_ref, win_ref, gv_ref, wsp_ref, bsp_ref,
                       ar_ref, ai_ref, wb_ref, wc_ref, wt_ref, dsk_ref, wglu_ref, bglu_ref, wout_ref,
                       xo_ref, hre_ref, him_ref, ps_scr, hs_scr, xc_scr, tril_scr):
    nb = x_ref.shape[0]
    assert nb == SUBLANES
    step = pl.program_id(0)

    @pl.when(step == 0)
    def _():
        hs_scr[0:SUBLANES, :] = jnp.zeros((SUBLANES, S_W), F32)

    row = lax.broadcasted_iota(jnp.int32, (CHUNK, CHUNK), 0)
    col = lax.broadcasted_iota(jnp.int32, (CHUNK, CHUNK), 1)
    for hd in range(HEADS):
        tril_scr[hd] = jnp.where(row >= col, wsp_ref[hd], 0.0).astype(BF16)

    def load_rows(c):
        xs, hs = [], []
        for bb in range(SEQ_PER_CHUNK):
            b = c * SEQ_PER_CHUNK + bb
            x = x_ref[b]
            xs.append(x)
            hs.append(_rms(x, g1_ref[...]) * (1.0 + sc_ref[pl.ds(b, 1), :]) + sh_ref[pl.ds(b, 1), :])
        return jnp.concatenate(xs, axis=0), jnp.concatenate(hs, axis=0).astype(BF16)

    def slab_rows(c, bb):
        return pl.ds(pl.multiple_of((c * SEQ_PER_CHUNK + bb) * PITCH, SUBLANES), SEQ_TILE)

    def phase_a(c, carry):
        _, h = load_rows(c)
        ps = _bdot(h, win_ref[:, 2 * D:3 * D])
        for bb in range(SEQ_PER_CHUNK):
            for k in range(SLABS):
                ps_scr[k, slab_rows(c, bb), :] = ps[bb * SEQ_TILE:(bb + 1) * SEQ_TILE, k * LANES:(k + 1) * LANES]
        return carry

    lax.fori_loop(0, nb // SEQ_PER_CHUNK, phase_a, 0, unroll=True)

    def phase_b(sub, carry):
        t0 = sub * SUB_T

        def step_rows(k, i):
            return [pl.ds(t0 + CHUNK_T * c + i, SUBLANES, stride=PITCH) for c in range(SUB_C)]

        for k in range(SLABS):
            xc = jnp.concatenate(
                [jnp.concatenate([ps_scr[k, r, :] for r in step_rows(k, i)], axis=0) for i in range(CHUNK_T)],
                axis=1).astype(BF16)
            xc_scr[k] = xc
            hs_scr[SUBLANES:SUBLANES + SUB_ROWS, k * SLAB_W:(k + 1) * SLAB_W] = jnp.dot(
                xc, wb_ref[k], preferred_element_type=F32)
        for j in range(PAIRS):
            re, im, half = _pair_lanes(j)
            ar = jnp.broadcast_to(ar_ref[:, half], (SUBLANES, LANES))
            ai = jnp.broadcast_to(ai_ref[:, half], (SUBLANES, LANES))
            ar, ai = ar * ar - ai * ai, 2.0 * (ar * ai)
            hr = hs_scr[0:SUBLANES, re]
            hi = hs_scr[0:SUBLANES, im]
            for c in range(SUB_C):
                r = slice(SUBLANES * (c + 1), SUBLANES * (c + 2))
                hr, hi = (ar * hr - ai * hi + hs_scr[r, re], ar * hi + ai * hr + hs_scr[r, im])
                hs_scr[r, re] = hr
                hs_scr[r, im] = hi
        for k in range(SLABS):
            h_prev = hs_scr[0:SUB_ROWS, k * SLAB_W:(k + 1) * SLAB_W]
            y2 = (jnp.dot(xc_scr[k], wt_ref[k], preferred_element_type=F32)
                  + jnp.dot(h_prev.astype(BF16), wc_ref[k], preferred_element_type=F32))
            d_skip = jnp.broadcast_to(dsk_ref[:, k * LANES:(k + 1) * LANES], (SUBLANES, LANES))
            for i in range(CHUNK_T):
                for c, r in enumerate(step_rows(k, i)):
                    ps_scr[k, r, :] = (y2[c * SUBLANES:(c + 1) * SUBLANES, i * LANES:(i + 1) * LANES]
                                       + d_skip * ps_scr[k, r, :])
        hs_scr[0:SUBLANES, :] = hs_scr[SUB_ROWS:SUB_ROWS + SUBLANES, :]
        return carry

    lax.fori_loop(0, SEQ_TILE // SUB_T, phase_b, 0, unroll=2)

    @pl.when(step == pl.num_programs(0) - 1)
    def _():
        for j in range(PAIRS):
            re, im, half = _pair_lanes(j)
            hre_ref[:, half] = hs_scr[0:SUBLANES, re]
            him_ref[:, half] = hs_scr[0:SUBLANES, im]

    def mixed_fn(v):
        vb = v.astype(BF16)
        rows = []
        for bb in range(SEQ_PER_CHUNK):
            heads = [jnp.dot(tril_scr[hd], vb[bb * CHUNK:(bb + 1) * CHUNK, hd * HEAD_DIM:(hd + 1) * HEAD_DIM],
                             preferred_element_type=F32) for hd in range(HEADS)]
            rows.append(jnp.concatenate(heads, axis=1) + bsp_ref[...])
        return jnp.concatenate(rows, axis=0)

    def phase_c(c, carry):
        xb, h = load_rows(c)
        y = jnp.concatenate(
            [jnp.concatenate([ps_scr[k, slab_rows(c, bb), :] for k in range(SLABS)], axis=1)
             for bb in range(SEQ_PER_CHUNK)], axis=0)
        gt = jnp.concatenate(
            [jnp.broadcast_to(gt_ref[pl.ds(c * SEQ_PER_CHUNK + bb, 1), :], (SEQ_TILE, D))
             for bb in range(SEQ_PER_CHUNK)], axis=0)
        xn, _ = _mix_back(xb, h, y, mixed_fn, gt, win_ref, gv_ref, wglu_ref, bglu_ref, wout_ref)
        for bb in range(SEQ_PER_CHUNK):
            xo_ref[c * SEQ_PER_CHUNK + bb] = xn[bb * SEQ_TILE:(bb + 1) * SEQ_TILE]
        return carry

    lax.fori_loop(0, nb // SEQ_PER_CHUNK, phase_c, 0, unroll=2)


def _mix_prompt(x, mod, mod_row_block, l, p):
    nb, seq, _ = x.shape
    assert nb == SUBLANES and seq % SEQ_TILE == 0
    tile = pl.BlockSpec((nb, SEQ_TILE, D), lambda s: (0, s, 0))
    consts = [p['g1'], p['w_in'], p['g_v'], p['w_sp'], p['b_sp'], p['a_re'], p['a_im'],
              p['wb'], p['wc'], p['wt'], p['d_skip'], p['w_glu'], p['b_glu'], p['w_out']]
    state = jax.ShapeDtypeStruct((nb, S_HALF), F32)
    state_spec = pl.BlockSpec((nb, S_HALF), lambda s: (0, 0))
    return pl.pallas_call(
        _mix_prompt_kernel,
        grid=(seq // SEQ_TILE,),
        in_specs=[tile] + _mod_specs(l, nb, mod_row_block, (0, 1, 2)) + [_layer_spec(a, l) for a in consts],
        out_specs=(tile, state_spec, state_spec),
        out_shape=(jax.ShapeDtypeStruct(x.shape, F32), state, state),
        scratch_shapes=[
            pltpu.VMEM((SLABS, nb * PITCH, LANES), F32),
            pltpu.VMEM((SUB_ROWS + SUBLANES, S_W), F32),
            pltpu.VMEM((SLABS, SUB_ROWS, CHUNK_T * LANES), BF16),
            pltpu.VMEM((HEADS, CHUNK, CHUNK), BF16),
        ],
        compiler_params=pltpu.CompilerParams(
            dimension_semantics=("arbitrary",), vmem_limit_bytes=VMEM_LIMIT),
        name="mix_prompt",
    )(x, mod, mod, mod, *consts)


def _mix_sample_kernel(x_ref, sh_ref, sc_ref, gt_ref, g1_ref, win_ref, gv_ref, wd_ref, bd_ref,
                       ar_ref, ai_ref, wb_ref, wc_ref, wt_ref, dsk_ref, wglu_ref, bglu_ref, wout_ref,
                       h0r_ref, h0i_ref, xo_ref, hre_ref, him_ref, v_ref):
    xb = x_ref[...]
    h = (_rms(xb, g1_ref[...]) * (1.0 + sc_ref[...]) + sh_ref[...]).astype(BF16)
    ps = _bdot(h, win_ref[:, 2 * D:3 * D])
    ys = []
    for k in range(SLABS):
        xk = ps[:, k * LANES:(k + 1) * LANES].astype(BF16)
        bu = jnp.dot(xk, wb_ref[k, LANES:2 * LANES, :], preferred_element_type=F32)
        h0 = []
        for q in range(SLAB_PAIRS):
            re, im, half = _pair_lanes(q)
            _, _, half = _pair_lanes(k * SLAB_PAIRS + q)
            ar, ai = ar_ref[:, half], ai_ref[:, half]
            hr, hi = h0r_ref[half, :].T, h0i_ref[half, :].T
            hre_ref[half, :] = (ar * hr - ai * hi + bu[:, re]).T
            him_ref[half, :] = (ar * hi + ai * hr + bu[:, im]).T
            h0 += [hr, hi]
        ys.append(_bdot(jnp.concatenate(h0, axis=1), wc_ref[k, :, 0:LANES])
                  + jnp.dot(xk, wt_ref[k, 0:LANES, 0:LANES], preferred_element_type=F32))
    mixed_fn = lambda v: wd_ref[...] * v + bd_ref[...]
    y = jnp.concatenate(ys, axis=1) + dsk_ref[...] * ps
    xn, v = _mix_back(xb, h, y, mixed_fn, gt_ref[...], win_ref, gv_ref, wglu_ref, bglu_ref, wout_ref)
    xo_ref[...] = xn
    v_ref[...] = v


def _mix_sample(x, mod, l, h0_re, h0_im, p):
    rows = x.shape[0]
    assert rows == LANES
    consts = [p['g1'], p['w_in'], p['g_v'], p['w_diag'], p['b_diag'], p['a_re'], p['a_im'],
              p['wb'], p['wc'], p['wt'], p['d_skip'], p['w_glu'], p['b_glu'], p['w_out'], h0_re, h0_im]
    whole = lambda shape: pl.BlockSpec(shape, lambda i: (0,) * len(shape))
    state = jax.ShapeDtypeStruct((S_HALF, rows), F32)
    return pl.pallas_call(
        _mix_sample_kernel,
        grid=(1,),
        in_specs=[whole((rows, D))] + _mod_specs(l, rows, 0, (0, 1, 2)) + [_layer_spec(a, l) for a in consts],
        out_specs=(whole((rows, D)), whole((S_HALF, rows)), whole((S_HALF, rows)), whole((rows, D))),
        out_shape=(jax.ShapeDtypeStruct((rows, D), F32), state, state, jax.ShapeDtypeStruct((rows, D), F32)),
        compiler_params=pltpu.CompilerParams(
            dimension_semantics=("arbitrary",), vmem_limit_bytes=VMEM_LIMIT),
        name="mix_sample",
    )(x, mod, mod, mod, *consts)


def _mlp_kernel(x_ref, sh_ref, sc_ref, gt_ref, g2_ref, w1_ref, w2_ref, gf_ref, xo_ref, *, final, tiles_per_mod):
    x = x_ref[...]
    if tiles_per_mod:
        b = pl.program_id(0) // tiles_per_mod
        sh, sc, gt = (r[pl.ds(b, 1), :] for r in (sh_ref, sc_ref, gt_ref))
    else:
        sh, sc, gt = sh_ref[...], sc_ref[...], gt_ref[...]
    h = (_rms(x, g2_ref[...]) * (1.0 + sc) + sh).astype(BF16)
    acc = jnp.zeros(x.shape, F32)
    for k in range(D_FF // D):
        a = jnp.dot(h, w1_ref[:, k * D:(k + 1) * D], preferred_element_type=F32)
        acc = acc + _bdot(jnp.square(jnp.maximum(a, 0.0)), w2_ref[k * D:(k + 1) * D, :])
    xn = x + gt * acc
    if final:
        xn = _rms(xn, gf_ref[...])
    xo_ref[...] = xn


def _mlp(x, mod, mod_row_block, l, p, g_final, final):
    if x.ndim == 3:
        nb, seq, _ = x.shape
        tm = min(MLP_ROWS, seq)
        assert seq % tm == 0
        per = seq // tm
        grid = (nb * per,)
        tile = pl.BlockSpec((None, tm, D), lambda i: (i // per, i % per, 0))
        mod_rows = nb
    else:
        per = 0
        grid = (1,)
        tile = pl.BlockSpec(x.shape, lambda i: (0, 0))
        mod_rows = x.shape[0]
    return pl.pallas_call(
        functools.partial(_mlp_kernel, final=final, tiles_per_mod=per),
        grid=grid,
        in_specs=[tile] + _mod_specs(l, mod_rows, mod_row_block, (3, 4, 5)) + [
            _layer_spec(p['g2'], l), _layer_spec(p['w_ff1'], l), _layer_spec(p['w_ff2'], l),
            pl.BlockSpec((1, D), lambda i: (0, 0))],
        out_specs=tile,
        out_shape=jax.ShapeDtypeStruct(x.shape, F32),
        compiler_params=pltpu.CompilerParams(
            dimension_semantics=("arbitrary",), vmem_limit_bytes=VMEM_LIMIT),
        name="mlp_final" if final else "mlp",
    )(x, mod, mod, mod, p['g2'], p['w_ff1'], p['w_ff2'], g_final)


def kernel(x_prompt, x_sample, c_prompt, c_sample, state_ssm_re, state_ssm_im, w_ada, b_ada, g_norm1, g_norm2, w_in, g_v, w_spatial, b_spatial, lam_re, lam_im, log_dt, b_re, b_im, c_re, c_im, d_skip, w_glu, b_glu, w_out, w_ff1, w_ff2, g_final):
    nbp, seq, _ = x_prompt.shape
    nbs = x_sample.shape[0]
    assert x_sample.shape[1] == 1 and nbs % nbp == 0

    a_re, a_im, bb_re, bb_im = _s5_prep(lam_re, lam_im, log_dt, b_re, b_im)
    wb, wc, wt = _s5_chunk_maps(*_s5_slab_layout(a_re, a_im, bb_re, bb_im, c_re, c_im))
    mod = _ada(jnp.concatenate([c_sample, c_prompt], axis=0), w_ada, b_ada)
    prompt_row_block = nbs // nbp

    vec = lambda a: a.reshape(DEPTH, 1, D)
    p = dict(
        g1=vec(g_norm1), g2=vec(g_norm2), g_v=vec(g_v), d_skip=vec(d_skip), b_glu=vec(b_glu),
        w_in=w_in.astype(BF16), w_glu=w_glu.astype(BF16), w_out=w_out.astype(BF16),
        w_ff1=w_ff1.astype(BF16), w_ff2=w_ff2.astype(BF16),
        w_sp=w_spatial, b_sp=jnp.repeat(jnp.swapaxes(b_spatial, 1, 2), HEAD_DIM, axis=2),
        w_diag=vec(jnp.repeat(w_spatial[:, :, 0, 0], HEAD_DIM, axis=1)),
        b_diag=vec(jnp.repeat(b_spatial[:, :, 0], HEAD_DIM, axis=1)),
        a_re=a_re.reshape(DEPTH, 1, S_HALF), a_im=a_im.reshape(DEPTH, 1, S_HALF), wb=wb, wc=wc, wt=wt)
    gf = g_final.reshape(1, D)
    lane_major = lambda s: jnp.transpose(s, (0, 2, 3, 1)).reshape(DEPTH, S_HALF, nbs)
    h0_re, h0_im = lane_major(state_ssm_re), lane_major(state_ssm_im)

    xp = x_prompt
    xs = x_sample.reshape(nbs, D)
    re_p, im_p, re_s, im_s, v_s = [], [], [], [], []
    for l in range(DEPTH):
        final = l == DEPTH - 1
        xp, hr, hi = _mix_prompt(xp, mod, prompt_row_block, l, p)
        xp = _mlp(xp, mod, prompt_row_block, l, p, gf, final)
        re_p.append(hr)
        im_p.append(hi)
        xs, hr, hi, v = _mix_sample(xs, mod, l, h0_re, h0_im, p)
        xs = _mlp(xs, mod, 0, l, p, gf, final)
        re_s.append(hr)
        im_s.append(hi)
        v_s.append(v)

    state = lambda hs: jnp.stack(hs).reshape(DEPTH, nbp, GROUPS, STATE)
    state_t = lambda hs: jnp.transpose(jnp.stack(hs).reshape(DEPTH, GROUPS, STATE, nbs), (0, 3, 1, 2))
    return (xp, xs.reshape(nbs, 1, D), state(re_p), state(im_p),
            state_t(re_s), state_t(im_s), jnp.stack(v_s).reshape(DEPTH, nbs, 1, D))
```

```python
import functools
import math

import jax
import jax.numpy as jnp
import numpy as np
from jax import lax
from jax.experimental import pallas as pl
from jax.experimental.pallas import tpu as pltpu

F32 = jnp.float32
BF16 = jnp.bfloat16

D = 1024
DEPTH = 2
CHUNK = 128
HEADS = 4
HEAD_DIM = D // HEADS
GROUPS = 64
GROUP_W = 16
STATE = 64
D_FF = 4 * D
N_IN = 5 * D
EPS = 1e-6

LANES = 128
SUBLANES = 8
PAIRS = GROUPS // 2
PAIR_W = 4 * STATE
S_W = PAIRS * PAIR_W
S_HALF = S_W // 2
SLABS = D // LANES
SLAB_PAIRS = LANES // (2 * GROUP_W)
SLAB_W = SLAB_PAIRS * PAIR_W
CHUNK_T = 2

SEQ_TILE = CHUNK
ROW_CHUNK = 256
SEQ_PER_CHUNK = ROW_CHUNK // SEQ_TILE
SUB_C = 32
SUB_ROWS = SUB_C * SUBLANES
SUB_T = SUB_C * CHUNK_T
PITCH = SEQ_TILE + SUBLANES
MLP_ROWS = 1024
VMEM_LIMIT = 61 * 1024 * 1024

_GELU_C = math.sqrt(2.0 / math.pi)
_LOG2E = math.log2(math.e)


def _gelu(x):
    t = jnp.tanh(x * (_GELU_C + (_GELU_C * 0.044715) * (x * x)))
    hx = 0.5 * x
    return hx + hx * t


def _sigmoid(x):
    return 1.0 / (1.0 + jnp.exp2(x * (-_LOG2E)))


def _rms(x, g):
    return x * lax.rsqrt(jnp.mean(x * x, axis=-1, keepdims=True) + EPS) * g


def _bdot(a, b):
    return jnp.dot(a.astype(BF16), b, preferred_element_type=F32)


def _layer_spec(a, l):
    zeros = (0,) * (a.ndim - 1)
    return pl.BlockSpec((None,) + a.shape[1:], lambda *_: (l,) + zeros, pipeline_mode=pl.Buffered(1))


def _s5_prep_kernel(lr_ref, li_ref, ldt_ref, lrw_ref, liw_ref, ldtw_ref, br_ref, bi_ref,
                    ar_ref, ai_ref, bbr_ref, bbi_ref):
    def a_bar(lr, li, ldt):
        dt = jnp.exp(ldt)
        mag = jnp.exp(lr * dt)
        return mag * jnp.cos(li * dt), mag * jnp.sin(li * dt)

    ar, ai = a_bar(lr_ref[...], li_ref[...], ldt_ref[...])
    ar_ref[...] = ar
    ai_ref[...] = ai
    lr, li = lrw_ref[...], liw_ref[...]
    ar, ai = a_bar(lr, li, ldtw_ref[...])
    nr, ni = ar - 1.0, ai
    den = lr * lr + li * li
    cr = (nr * lr + ni * li) / den
    ci = (ni * lr - nr * li) / den
    br, bi = br_ref[...], bi_ref[...]
    bbr_ref[...] = cr * br - ci * bi
    bbi_ref[...] = cr * bi + ci * br


def _s5_prep(lam_re, lam_im, log_dt, b_re, b_im):
    wide = (DEPTH, GROUPS, STATE * GROUP_W)
    ldt = jnp.broadcast_to(log_dt[:, :, None], lam_re.shape)
    lam_w = jnp.repeat(jnp.stack([lam_re, lam_im, ldt]), GROUP_W, axis=-1)
    small = jax.ShapeDtypeStruct(lam_re.shape, F32)
    big = jax.ShapeDtypeStruct(wide, F32)
    return pl.pallas_call(
        _s5_prep_kernel, out_shape=(small, small, big, big), name="s5_prep",
    )(lam_re, lam_im, ldt, lam_w[0], lam_w[1], lam_w[2], b_re.reshape(wide), b_im.reshape(wide))


def _s5_slab_layout(a_re, a_im, bb_re, bb_im, c_re, c_im):
    bb = jnp.stack([bb_re, bb_im]).reshape(2, DEPTH, SLABS, SLAB_PAIRS, 2, STATE, GROUP_W)
    b = jnp.transpose(bb, (1, 2, 3, 4, 6, 0, 5)).reshape(DEPTH, SLABS, LANES, 2 * STATE)
    cc = jnp.stack([c_re, -c_im]).reshape(2, DEPTH, SLABS, LANES, STATE)
    ct = jnp.transpose(cc, (1, 2, 3, 0, 4)).reshape(DEPTH, SLABS, LANES, 2 * STATE)

    def lanes(a):
        a = a.reshape(DEPTH, SLABS, SLAB_PAIRS, 1, 2 * STATE)
        return jnp.broadcast_to(a, (DEPTH, SLABS, SLAB_PAIRS, 2, 2 * STATE)).reshape(DEPTH, SLABS, 1, SLAB_W)

    return b, ct, lanes(a_re), lanes(a_im)


def _spread_matrix():
    spread = np.zeros((2, STATE, SLAB_PAIRS, 2, 2, STATE), np.float32)
    for r in range(2):
        spread[r, np.arange(STATE), :, r, :, np.arange(STATE)] = 1.0
    return jnp.asarray(spread.reshape(2 * STATE, SLAB_W), BF16)


def _cmul_cols(m, ar_row, ai_row, sign):
    parts = []
    for q in range(SLAB_PAIRS):
        re = slice(q * PAIR_W, q * PAIR_W + LANES)
        im = slice(q * PAIR_W + LANES, (q + 1) * PAIR_W)
        ar, ai = ar_row[:, re], sign * ai_row[:, re]
        parts += [m[:, re] * ar - m[:, im] * ai, m[:, im] * ar + m[:, re] * ai]
    return jnp.concatenate(parts, axis=1)


def _s5_chunk_kernel(b_ref, ct_ref, spread_ref, ar_ref, ai_ref, wb_ref, wc_ref, wt_ref):
    lane = lax.broadcasted_iota(jnp.int32, (LANES, SLAB_W), 1)
    chan = lax.broadcasted_iota(jnp.int32, (LANES, SLAB_W), 0)
    same_group = (lane // PAIR_W) * 2 + (lane // STATE) % 2 == chan // GROUP_W
    b = jnp.where(same_group, _bdot(b_ref[...], spread_ref[...]), 0.0)
    ct = jnp.where(same_group, _bdot(ct_ref[...], spread_ref[...]), 0.0)
    ar, ai = ar_ref[...], ai_ref[...]
    ab = _cmul_cols(b, ar, ai, 1.0)
    wb_ref[0:LANES, :] = ab.astype(BF16)
    wb_ref[LANES:2 * LANES, :] = b.astype(BF16)
    cat = _cmul_cols(ct, ar, ai, -1.0)
    wc_ref[:, 0:LANES] = cat.T.astype(BF16)
    wc_ref[:, LANES:2 * LANES] = _cmul_cols(cat, ar, ai, -1.0).T.astype(BF16)
    ctb = ct.astype(BF16)
    times_c = lambda m: lax.dot_general(m.astype(BF16), ctb, (((1,), (1,)), ((), ())),
                                        preferred_element_type=F32).astype(BF16)
    k0 = times_c(b)
    wt_ref[0:LANES, 0:LANES] = k0
    wt_ref[0:LANES, LANES:2 * LANES] = times_c(ab)
    wt_ref[LANES:2 * LANES, 0:LANES] = jnp.zeros((LANES, LANES), BF16)
    wt_ref[LANES:2 * LANES, LANES:2 * LANES] = k0


def _s5_chunk_maps(b, ct, a_re_lanes, a_im_lanes):
    spread = _spread_matrix()
    blk = lambda r, c: pl.BlockSpec((None, None, r, c), lambda l, k: (l, k, 0, 0))
    out = lambda r, c: jax.ShapeDtypeStruct((DEPTH, SLABS, r, c), BF16)
    return pl.pallas_call(
        _s5_chunk_kernel,
        grid=(DEPTH, SLABS),
        in_specs=[blk(LANES, 2 * STATE), blk(LANES, 2 * STATE), pl.BlockSpec(spread.shape, lambda l, k: (0, 0)),
                  blk(1, SLAB_W), blk(1, SLAB_W)],
        out_specs=(blk(2 * LANES, SLAB_W), blk(SLAB_W, 2 * LANES), blk(2 * LANES, 2 * LANES)),
        out_shape=(out(2 * LANES, SLAB_W), out(SLAB_W, 2 * LANES), out(2 * LANES, 2 * LANES)),
        name="s5_chunk_maps",
    )(b, ct, spread, a_re_lanes, a_im_lanes)


ADA_TILE = 1536


def _ada_kernel(c_ref, w_ref, b_ref, o_ref):
    c = c_ref[...]
    o_ref[...] = _bdot(c * _sigmoid(c), w_ref[...].astype(BF16)) + b_ref[...]


def _ada(c_all, w_ada, b_ada):
    rows = c_all.shape[0]
    return pl.pallas_call(
        _ada_kernel,
        grid=(DEPTH, 6 * D // ADA_TILE),
        in_specs=[
            pl.BlockSpec((rows, D), lambda l, n: (0, 0)),
            pl.BlockSpec((None, D, ADA_TILE), lambda l, n: (l, 0, n)),
            pl.BlockSpec((None, 1, ADA_TILE), lambda l, n: (l, 0, n)),
        ],
        out_specs=pl.BlockSpec((None, rows, ADA_TILE), lambda l, n: (l, 0, n)),
        out_shape=jax.ShapeDtypeStruct((DEPTH, rows, 6 * D), F32),
        name="adaln",
    )(c_all, w_ada, b_ada.reshape(DEPTH, 1, 6 * D))


def _mod_specs(l, rows, row_block, which):
    return [pl.BlockSpec((None, rows, D), lambda *_, n=n: (l, row_block, n)) for n in which]


def _mix_back(xb, h, y, mixed_fn, gt, win_ref, gv_ref, wglu_ref, bglu_ref, wout_ref):
    v = _rms(_gelu(_bdot(h, win_ref[:, D:2 * D])), gv_ref[...])
    y_a = _gelu(_bdot(h, win_ref[:, 0:D])) * mixed_fn(v)
    merged = _sigmoid(_bdot(h, win_ref[:, 3 * D:4 * D])) * y_a
    z = _gelu(y)
    y_b = z * _sigmoid(_bdot(z, wglu_ref[...]) + bglu_ref[...])
    merged = merged + _sigmoid(_bdot(h, win_ref[:, 4 * D:5 * D])) * y_b
    return xb + gt * _bdot(merged, wout_ref[...]), v


MIX_WEIGHTS = ('w_in', 'wb', 'wc', 'wt', 'w_glu', 'w_out')
W_S5_IN, W_S5_MAPS, W_REST = (0,), (1, 2, 3), (4, 5, 6, 7)


def _mix_weight_copies(l, hbm_refs, vmem_refs, sems):
    win_h, wb_h, wc_h, wt_h, wglu_h, wout_h = hbm_refs
    win_v, wb_v, wc_v, wt_v, wglu_v, wout_v = vmem_refs
    cols = lambda lo, hi: (win_h.at[l, :, lo * D:hi * D], win_v.at[:, lo * D:hi * D])
    pieces = [cols(2, 3), (wb_h.at[l], wb_v), (wt_h.at[l], wt_v), (wc_h.at[l], wc_v),
              cols(0, 2), cols(3, 5), (wglu_h.at[l], wglu_v), (wout_h.at[l], wout_v)]
    return [pltpu.make_async_copy(src, dst, sems.at[i]) for i, (src, dst) in enumerate(pieces)]


def _mix_weight_scratch(p):
    return [pltpu.VMEM(p[name].shape[1:], BF16) for name in MIX_WEIGHTS] + [pltpu.SemaphoreType.DMA((8,))]


def _pair_lanes(j):
    re = slice(j * PAIR_W, j * PAIR_W + LANES)
    im = slice(j * PAIR_W + LANES, (j + 1) * PAIR_W)
    half = slice(j * LANES, (j + 1) * LANES)
    return re, im, half


def _mix_prompt_kernel(x_ref, sh_ref, sc_ref, gt_ref, g1_ref, gv_ref, wsp_ref, bsp_ref,
                       ar_ref, ai_ref, dsk_ref, bglu_ref,
                       win_hbm, wb_hbm, wc_hbm, wt_hbm, wglu_hbm, wout_hbm,
                       xo_ref, hre_ref, him_ref, ps_scr, hs_scr, xc_scr, tril_scr,
                       win_ref, wb_ref, wc_ref, wt_ref, wglu_ref, wout_ref, w_sems, *, layer):
    nb = x_ref.shape[0]
    assert nb == SUBLANES
    step = pl.program_id(0)

    copies = _mix_weight_copies(layer, (win_hbm, wb_hbm, wc_hbm, wt_hbm, wglu_hbm, wout_hbm),
                                (win_ref, wb_ref, wc_ref, wt_ref, wglu_ref, wout_ref), w_sems)

    def await_weights(group):
        @pl.when(step == 0)
        def _():
            for i in group:
                copies[i].wait()

    @pl.when(step == 0)
    def _():
        for c in copies:
            c.start()
        hs_scr[0:SUBLANES, :] = jnp.zeros((SUBLANES, S_W), F32)

    row = lax.broadcasted_iota(jnp.int32, (CHUNK, CHUNK), 0)
    col = lax.broadcasted_iota(jnp.int32, (CHUNK, CHUNK), 1)
    for hd in range(HEADS):
        tril_scr[hd] = jnp.where(row >= col, wsp_ref[hd], 0.0).astype(BF16)

    def load_rows(c):
        xs, hs = [], []
        for bb in range(SEQ_PER_CHUNK):
            b = c * SEQ_PER_CHUNK + bb
            x = x_ref[b]
            xs.append(x)
            hs.append(_rms(x, g1_ref[...]) * (1.0 + sc_ref[pl.ds(b, 1), :]) + sh_ref[pl.ds(b, 1), :])
        return jnp.concatenate(xs, axis=0), jnp.concatenate(hs, axis=0).astype(BF16)

    def slab_rows(c, bb):
        return pl.ds(pl.multiple_of((c * SEQ_PER_CHUNK + bb) * PITCH, SUBLANES), SEQ_TILE)

    def phase_a(c, carry):
        _, h = load_rows(c)
        ps = _bdot(h, win_ref[:, 2 * D:3 * D])
        for bb in range(SEQ_PER_CHUNK):
            for k in range(SLABS):
                ps_scr[k, slab_rows(c, bb), :] = ps[bb * SEQ_TILE:(bb + 1) * SEQ_TILE, k * LANES:(k + 1) * LANES]
        return carry

    await_weights(W_S5_IN)
    lax.fori_loop(0, nb // SEQ_PER_CHUNK, phase_a, 0, unroll=True)

    def phase_b(sub, carry):
        t0 = sub * SUB_T

        def step_rows(k, i):
            return [pl.ds(t0 + CHUNK_T * c + i, SUBLANES, stride=PITCH) for c in range(SUB_C)]

        for k in range(SLABS):
            xc = jnp.concatenate(
                [jnp.concatenate([ps_scr[k, r, :] for r in step_rows(k, i)], axis=0) for i in range(CHUNK_T)],
                axis=1).astype(BF16)
            xc_scr[k] = xc
            hs_scr[SUBLANES:SUBLANES + SUB_ROWS, k * SLAB_W:(k + 1) * SLAB_W] = jnp.dot(
                xc, wb_ref[k], preferred_element_type=F32)
        for j in range(PAIRS):
            re, im, half = _pair_lanes(j)
            ar = jnp.broadcast_to(ar_ref[:, half], (SUBLANES, LANES))
            ai = jnp.broadcast_to(ai_ref[:, half], (SUBLANES, LANES))
            ar, ai = ar * ar - ai * ai, 2.0 * (ar * ai)
            hr = hs_scr[0:SUBLANES, re]
            hi = hs_scr[0:SUBLANES, im]
            for c in range(SUB_C):
                r = slice(SUBLANES * (c + 1), SUBLANES * (c + 2))
                hr, hi = (ar * hr - ai * hi + hs_scr[r, re], ar * hi + ai * hr + hs_scr[r, im])
                hs_scr[r, re] = hr
                hs_scr[r, im] = hi
        for k in range(SLABS):
            h_prev = hs_scr[0:SUB_ROWS, k * SLAB_W:(k + 1) * SLAB_W]
            y2 = (jnp.dot(xc_scr[k], wt_ref[k], preferred_element_type=F32)
                  + jnp.dot(h_prev.astype(BF16), wc_ref[k], preferred_element_type=F32))
            d_skip = jnp.broadcast_to(dsk_ref[:, k * LANES:(k + 1) * LANES], (SUBLANES, LANES))
            for i in range(CHUNK_T):
                for c, r in enumerate(step_rows(k, i)):
                    ps_scr[k, r, :] = (y2[c * SUBLANES:(c + 1) * SUBLANES, i * LANES:(i + 1) * LANES]
                                       + d_skip * ps_scr[k, r, :])
        hs_scr[0:SUBLANES, :] = hs_scr[SUB_ROWS:SUB_ROWS + SUBLANES, :]
        return carry

    await_weights(W_S5_MAPS)
    lax.fori_loop(0, SEQ_TILE // SUB_T, phase_b, 0, unroll=2)

    @pl.when(step == pl.num_programs(0) - 1)
    def _():
        for j in range(PAIRS):
            re, im, half = _pair_lanes(j)
            hre_ref[:, half] = hs_scr[0:SUBLANES, re]
            him_ref[:, half] = hs_scr[0:SUBLANES, im]

    def mixed_fn(v):
        vb = v.astype(BF16)
        rows = []
        for bb in range(SEQ_PER_CHUNK):
            heads = [jnp.dot(tril_scr[hd], vb[bb * CHUNK:(bb + 1) * CHUNK, hd * HEAD_DIM:(hd + 1) * HEAD_DIM],
                             preferred_element_type=F32) for hd in range(HEADS)]
            rows.append(jnp.concatenate(heads, axis=1) + bsp_ref[...])
        return jnp.concatenate(rows, axis=0)

    def phase_c(c, carry):
        xb, h = load_rows(c)
        y = jnp.concatenate(
            [jnp.concatenate([ps_scr[k, slab_rows(c, bb), :] for k in range(SLABS)], axis=1)
             for bb in range(SEQ_PER_CHUNK)], axis=0)
        gt = jnp.concatenate(
            [jnp.broadcast_to(gt_ref[pl.ds(c * SEQ_PER_CHUNK + bb, 1), :], (SEQ_TILE, D))
             for bb in range(SEQ_PER_CHUNK)], axis=0)
        xn, _ = _mix_back(xb, h, y, mixed_fn, gt, win_ref, gv_ref, wglu_ref, bglu_ref, wout_ref)
        for bb in range(SEQ_PER_CHUNK):
            xo_ref[c * SEQ_PER_CHUNK + bb] = xn[bb * SEQ_TILE:(bb + 1) * SEQ_TILE]
        return carry

    await_weights(W_REST)
    lax.fori_loop(0, nb // SEQ_PER_CHUNK, phase_c, 0, unroll=2)


def _mix_prompt(x, mod, mod_row_block, l, p):
    nb, seq, _ = x.shape
    assert nb == SUBLANES and seq % SEQ_TILE == 0
    tile = pl.BlockSpec((nb, SEQ_TILE, D), lambda s: (0, s, 0))
    consts = [p['g1'], p['g_v'], p['w_sp'], p['b_sp'], p['a_re'], p['a_im'], p['d_skip'], p['b_glu']]
    weights = [p[name] for name in MIX_WEIGHTS]
    state = jax.ShapeDtypeStruct((nb, S_HALF), F32)
    state_spec = pl.BlockSpec((nb, S_HALF), lambda s: (0, 0))
    return pl.pallas_call(
        functools.partial(_mix_prompt_kernel, layer=l),
        grid=(seq // SEQ_TILE,),
        in_specs=([tile] + _mod_specs(l, nb, mod_row_block, (0, 1, 2)) + [_layer_spec(a, l) for a in consts]
                  + [pl.BlockSpec(memory_space=pl.ANY)] * len(weights)),
        out_specs=(tile, state_spec, state_spec),
        out_shape=(jax.ShapeDtypeStruct(x.shape, F32), state, state),
        scratch_shapes=[
            pltpu.VMEM((SLABS, nb * PITCH, LANES), F32),
            pltpu.VMEM((SUB_ROWS + SUBLANES, S_W), F32),
            pltpu.VMEM((SLABS, SUB_ROWS, CHUNK_T * LANES), BF16),
            pltpu.VMEM((HEADS, CHUNK, CHUNK), BF16),
        ] + _mix_weight_scratch(p),
        compiler_params=pltpu.CompilerParams(
            dimension_semantics=("arbitrary",), vmem_limit_bytes=VMEM_LIMIT),
        name="mix_prompt",
    )(x, mod, mod, mod, *consts, *weights)


def _mix_sample_kernel(x_ref, sh_ref, sc_ref, gt_ref, g1_ref, gv_ref, wd_ref, bd_ref,
                       ar_ref, ai_ref, dsk_ref, bglu_ref, h0r_ref, h0i_ref,
                       win_hbm, wb_hbm, wc_hbm, wt_hbm, wglu_hbm, wout_hbm,
                       xo_ref, hre_ref, him_ref, v_ref,
                       win_ref, wb_ref, wc_ref, wt_ref, wglu_ref, wout_ref, w_sems, *, layer):
    copies = _mix_weight_copies(layer, (win_hbm, wb_hbm, wc_hbm, wt_hbm, wglu_hbm, wout_hbm),
                                (win_ref, wb_ref, wc_ref, wt_ref, wglu_ref, wout_ref), w_sems)
    for c in copies:
        c.start()
    xb = x_ref[...]
    h = (_rms(xb, g1_ref[...]) * (1.0 + sc_ref[...]) + sh_ref[...]).astype(BF16)
    for i in W_S5_IN:
        copies[i].wait()
    ps = _bdot(h, win_ref[:, 2 * D:3 * D])
    for i in W_S5_MAPS:
        copies[i].wait()
    ys = []
    for k in range(SLABS):
        xk = ps[:, k * LANES:(k + 1) * LANES].astype(BF16)
        bu = jnp.dot(xk, wb_ref[k, LANES:2 * LANES, :], preferred_element_type=F32)
        h0 = []
        for q in range(SLAB_PAIRS):
            re, im, half = _pair_lanes(q)
            _, _, half = _pair_lanes(k * SLAB_PAIRS + q)
            ar, ai = ar_ref[:, half], ai_ref[:, half]
            hr, hi = h0r_ref[half, :].T, h0i_ref[half, :].T
            hre_ref[half, :] = (ar * hr - ai * hi + bu[:, re]).T
            him_ref[half, :] = (ar * hi + ai * hr + bu[:, im]).T
            h0 += [hr, hi]
        ys.append(_bdot(jnp.concatenate(h0, axis=1), wc_ref[k, :, 0:LANES])
                  + jnp.dot(xk, wt_ref[k, 0:LANES, 0:LANES], preferred_element_type=F32))
    mixed_fn = lambda v: wd_ref[...] * v + bd_ref[...]
    y = jnp.concatenate(ys, axis=1) + dsk_ref[...] * ps
    for i in W_REST:
        copies[i].wait()
    xn, v = _mix_back(xb, h, y, mixed_fn, gt_ref[...], win_ref, gv_ref, wglu_ref, bglu_ref, wout_ref)
    xo_ref[...] = xn
    v_ref[...] = v


def _mix_sample(x, mod, l, h0_re, h0_im, p):
    rows = x.shape[0]
    assert rows == LANES
    consts = [p['g1'], p['g_v'], p['w_diag'], p['b_diag'], p['a_re'], p['a_im'], p['d_skip'], p['b_glu'],
              h0_re, h0_im]
    weights = [p[name] for name in MIX_WEIGHTS]
    whole = lambda shape: pl.BlockSpec(shape, lambda i: (0,) * len(shape))
    state = jax.ShapeDtypeStruct((S_HALF, rows), F32)
    return pl.pallas_call(
        functools.partial(_mix_sample_kernel, layer=l),
        grid=(1,),
        in_specs=([whole((rows, D))] + _mod_specs(l, rows, 0, (0, 1, 2)) + [_layer_spec(a, l) for a in consts]
                  + [pl.BlockSpec(memory_space=pl.ANY)] * len(weights)),
        out_specs=(whole((rows, D)), whole((S_HALF, rows)), whole((S_HALF, rows)), whole((rows, D))),
        out_shape=(jax.ShapeDtypeStruct((rows, D), F32), state, state, jax.ShapeDtypeStruct((rows, D), F32)),
        scratch_shapes=_mix_weight_scratch(p),
        compiler_params=pltpu.CompilerParams(
            dimension_semantics=("arbitrary",), vmem_limit_bytes=VMEM_LIMIT),
        name="mix_sample",
    )(x, mod, mod, mod, *consts, *weights)


def _mlp_kernel(x_ref, sh_ref, sc_ref, gt_ref, g2_ref, w1_ref, w2_ref, gf_ref, xo_ref, *, final, tiles_per_mod):
    x = x_ref[...]
    if tiles_per_mod:
        b = pl.program_id(0) // tiles_per_mod
        sh, sc, gt = (r[pl.ds(b, 1), :] for r in (sh_ref, sc_ref, gt_ref))
    else:
        sh, sc, gt = sh_ref[...], sc_ref[...], gt_ref[...]
    h = (_rms(x, g2_ref[...]) * (1.0 + sc) + sh).astype(BF16)
    acc = jnp.zeros(x.shape, F32)
    for k in range(D_FF // D):
        a = jnp.dot(h, w1_ref[:, k * D:(k + 1) * D], preferred_element_type=F32)
        acc = acc + _bdot(jnp.square(jnp.maximum(a, 0.0)), w2_ref[k * D:(k + 1) * D, :])
    xn = x + gt * acc
    if final:
        xn = _rms(xn, gf_ref[...])
    xo_ref[...] = xn


def _mlp(x, mod, mod_row_block, l, p, g_final, final):
    if x.ndim == 3:
        nb, seq, _ = x.shape
        tm = min(MLP_ROWS, seq)
        assert seq % tm == 0
        per = seq // tm
        grid = (nb * per,)
        tile = pl.BlockSpec((None, tm, D), lambda i: (i // per, i % per, 0))
        mod_rows = nb
    else:
        per = 0
        grid = (1,)
        tile = pl.BlockSpec(x.shape, lambda i: (0, 0))
        mod_rows = x.shape[0]
    return pl.pallas_call(
        functools.partial(_mlp_kernel, final=final, tiles_per_mod=per),
        grid=grid,
        in_specs=[tile] + _mod_specs(l, mod_rows, mod_row_block, (3, 4, 5)) + [
            _layer_spec(p['g2'], l), _layer_spec(p['w_ff1'], l), _layer_spec(p['w_ff2'], l),
            pl.BlockSpec((1, D), lambda i: (0, 0))],
        out_specs=tile,
        out_shape=jax.ShapeDtypeStruct(x.shape, F32),
        compiler_params=pltpu.CompilerParams(
            dimension_semantics=("arbitrary",), vmem_limit_bytes=VMEM_LIMIT),
        name="mlp_final" if final else "mlp",
    )(x, mod, mod, mod, p['g2'], p['w_ff1'], p['w_ff2'], g_final)


def kernel(x_prompt, x_sample, c_prompt, c_sample, state_ssm_re, state_ssm_im, w_ada, b_ada, g_norm1, g_norm2, w_in, g_v, w_spatial, b_spatial, lam_re, lam_im, log_dt, b_re, b_im, c_re, c_im, d_skip, w_glu, b_glu, w_out, w_ff1, w_ff2, g_final):
    nbp, seq, _ = x_prompt.shape
    nbs = x_sample.shape[0]
    assert x_sample.shape[1] == 1 and nbs % nbp == 0

    a_re, a_im, bb_re, bb_im = _s5_prep(lam_re, lam_im, log_dt, b_re, b_im)
    wb, wc, wt = _s5_chunk_maps(*_s5_slab_layout(a_re, a_im, bb_re, bb_im, c_re, c_im))
    mod = _ada(jnp.concatenate([c_sample, c_prompt], axis=0), w_ada, b_ada)
    prompt_row_block = nbs // nbp

    vec = lambda a: a.reshape(DEPTH, 1, D)
    p = dict(
        g1=vec(g_norm1), g2=vec(g_norm2), g_v=vec(g_v), d_skip=vec(d_skip), b_glu=vec(b_glu),
        w_in=w_in.astype(BF16), w_glu=w_glu.astype(BF16), w_out=w_out.astype(BF16),
        w_ff1=w_ff1.astype(BF16), w_ff2=w_ff2.astype(BF16),
        w_sp=w_spatial, b_sp=jnp.repeat(jnp.swapaxes(b_spatial, 1, 2), HEAD_DIM, axis=2),
        w_diag=vec(jnp.repeat(w_spatial[:, :, 0, 0], HEAD_DIM, axis=1)),
        b_diag=vec(jnp.repeat(b_spatial[:, :, 0], HEAD_DIM, axis=1)),
        a_re=a_re.reshape(DEPTH, 1, S_HALF), a_im=a_im.reshape(DEPTH, 1, S_HALF), wb=wb, wc=wc, wt=wt)
    gf = g_final.reshape(1, D)
    lane_major = lambda s: jnp.transpose(s, (0, 2, 3, 1)).reshape(DEPTH, S_HALF, nbs)
    h0_re, h0_im = lane_major(state_ssm_re), lane_major(state_ssm_im)

    xp = x_prompt
    xs = x_sample.reshape(nbs, D)
    re_p, im_p, re_s, im_s, v_s = [], [], [], [], []
    for l in range(DEPTH):
        final = l == DEPTH - 1
        xp, hr, hi = _mix_prompt(xp, mod, prompt_row_block, l, p)
        xp = _mlp(xp, mod, prompt_row_block, l, p, gf, final)
        re_p.append(hr)
        im_p.append(hi)
        xs, hr, hi, v = _mix_sample(xs, mod, l, h0_re, h0_im, p)
        xs = _mlp(xs, mod, 0, l, p, gf, final)
        re_s.append(hr)
        im_s.append(hi)
        v_s.append(v)

    state = lambda hs: jnp.stack(hs).reshape(DEPTH, nbp, GROUPS, STATE)
    state_t = lambda hs: jnp.transpose(jnp.stack(hs).reshape(DEPTH, GROUPS, STATE, nbs), (0, 3, 1, 2))
    return (xp, xs.reshape(nbs, 1, D), state(re_p), state(im_p),
            state_t(re_s), state_t(im_s), jnp.stack(v_s).reshape(DEPTH, nbs, 1, D))
```

```python
import functools
import math

import jax
import jax.numpy as jnp
import numpy as np
from jax import lax
from jax.experimental import pallas as pl
from jax.experimental.pallas import tpu as pltpu

F32 = jnp.float32
BF16 = jnp.bfloat16

D = 1024
DEPTH = 2
CHUNK = 128
HEADS = 4
HEAD_DIM = D // HEADS
GROUPS = 64
GROUP_W = 16
STATE = 64
D_FF = 4 * D
N_IN = 5 * D
EPS = 1e-6

LANES = 128
SUBLANES = 8
PAIRS = GROUPS // 2
PAIR_W = 4 * STATE
S_W = PAIRS * PAIR_W
S_HALF = S_W // 2
SLABS = D // LANES
SLAB_PAIRS = LANES // (2 * GROUP_W)
SLAB_W = SLAB_PAIRS * PAIR_W
CHUNK_T = 2

SEQ_TILE = CHUNK
ROW_CHUNK = 256
SEQ_PER_CHUNK = ROW_CHUNK // SEQ_TILE
SUB_C = 32
SUB_ROWS = SUB_C * SUBLANES
SUB_T = SUB_C * CHUNK_T
PITCH = SEQ_TILE + SUBLANES
MLP_ROWS = 1024
VMEM_LIMIT = 61 * 1024 * 1024

_GELU_C = math.sqrt(2.0 / math.pi)
_LOG2E = math.log2(math.e)


def _gelu(x):
    t = jnp.tanh(x * (_GELU_C + (_GELU_C * 0.044715) * (x * x)))
    hx = 0.5 * x
    return hx + hx * t


def _sigmoid(x):
    return 1.0 / (1.0 + jnp.exp2(x * (-_LOG2E)))


def _rms(x, g):
    return x * lax.rsqrt(jnp.mean(x * x, axis=-1, keepdims=True) + EPS) * g


def _bdot(a, b):
    return jnp.dot(a.astype(BF16), b, preferred_element_type=F32)


def _layer_spec(a, l):
    zeros = (0,) * (a.ndim - 1)
    return pl.BlockSpec((None,) + a.shape[1:], lambda *_: (l,) + zeros, pipeline_mode=pl.Buffered(1))


def _s5_prep_kernel(lr_ref, li_ref, ldt_ref, lrw_ref, liw_ref, ldtw_ref, br_ref, bi_ref,
                    ar_ref, ai_ref, bbr_ref, bbi_ref):
    def a_bar(lr, li, ldt):
        dt = jnp.exp(ldt)
        mag = jnp.exp(lr * dt)
        return mag * jnp.cos(li * dt), mag * jnp.sin(li * dt)

    ar, ai = a_bar(lr_ref[...], li_ref[...], ldt_ref[...])
    ar_ref[...] = ar
    ai_ref[...] = ai
    lr, li = lrw_ref[...], liw_ref[...]
    ar, ai = a_bar(lr, li, ldtw_ref[...])
    nr, ni = ar - 1.0, ai
    den = lr * lr + li * li
    cr = (nr * lr + ni * li) / den
    ci = (ni * lr - nr * li) / den
    br, bi = br_ref[...], bi_ref[...]
    bbr_ref[...] = cr * br - ci * bi
    bbi_ref[...] = cr * bi + ci * br


def _s5_prep(lam_re, lam_im, log_dt, b_re, b_im):
    wide = (DEPTH, GROUPS, STATE * GROUP_W)
    ldt = jnp.broadcast_to(log_dt[:, :, None], lam_re.shape)
    lam_w = jnp.repeat(jnp.stack([lam_re, lam_im, ldt]), GROUP_W, axis=-1)
    small = jax.ShapeDtypeStruct(lam_re.shape, F32)
    big = jax.ShapeDtypeStruct(wide, F32)
    return pl.pallas_call(
        _s5_prep_kernel, out_shape=(small, small, big, big), name="s5_prep",
    )(lam_re, lam_im, ldt, lam_w[0], lam_w[1], lam_w[2], b_re.reshape(wide), b_im.reshape(wide))


def _s5_slab_layout(a_re, a_im, bb_re, bb_im, c_re, c_im):
    bb = jnp.stack([bb_re, bb_im]).reshape(2, DEPTH, SLABS, SLAB_PAIRS, 2, STATE, GROUP_W)
    b = jnp.transpose(bb, (1, 2, 3, 4, 6, 0, 5)).reshape(DEPTH, SLABS, LANES, 2 * STATE)
    cc = jnp.stack([c_re, -c_im]).reshape(2, DEPTH, SLABS, LANES, STATE)
    ct = jnp.transpose(cc, (1, 2, 3, 0, 4)).reshape(DEPTH, SLABS, LANES, 2 * STATE)

    def lanes(a):
        a = a.reshape(DEPTH, SLABS, SLAB_PAIRS, 1, 2 * STATE)
        return jnp.broadcast_to(a, (DEPTH, SLABS, SLAB_PAIRS, 2, 2 * STATE)).reshape(DEPTH, SLABS, 1, SLAB_W)

    return b, ct, lanes(a_re), lanes(a_im)


def _spread_matrix():
    spread = np.zeros((2, STATE, SLAB_PAIRS, 2, 2, STATE), np.float32)
    for r in range(2):
        spread[r, np.arange(STATE), :, r, :, np.arange(STATE)] = 1.0
    return jnp.asarray(spread.reshape(2 * STATE, SLAB_W), BF16)


def _cmul_cols(m, ar_row, ai_row, sign):
    parts = []
    for q in range(SLAB_PAIRS):
        re = slice(q * PAIR_W, q * PAIR_W + LANES)
        im = slice(q * PAIR_W + LANES, (q + 1) * PAIR_W)
        ar, ai = ar_row[:, re], sign * ai_row[:, re]
        parts += [m[:, re] * ar - m[:, im] * ai, m[:, im] * ar + m[:, re] * ai]
    return jnp.concatenate(parts, axis=1)


def _s5_chunk_kernel(b_ref, ct_ref, spread_ref, ar_ref, ai_ref, wb_ref, wc_ref, wt_ref):
    lane = lax.broadcasted_iota(jnp.int32, (LANES, SLAB_W), 1)
    chan = lax.broadcasted_iota(jnp.int32, (LANES, SLAB_W), 0)
    same_group = (lane // PAIR_W) * 2 + (lane // STATE) % 2 == chan // GROUP_W
    b = jnp.where(same_group, _bdot(b_ref[...], spread_ref[...]), 0.0)
    ct = jnp.where(same_group, _bdot(ct_ref[...], spread_ref[...]), 0.0)
    ar, ai = ar_ref[...], ai_ref[...]
    ab = _cmul_cols(b, ar, ai, 1.0)
    wb_ref[0:LANES, :] = ab.astype(BF16)
    wb_ref[LANES:2 * LANES, :] = b.astype(BF16)
    cat = _cmul_cols(ct, ar, ai, -1.0)
    wc_ref[:, 0:LANES] = cat.T.astype(BF16)
    wc_ref[:, LANES:2 * LANES] = _cmul_cols(cat, ar, ai, -1.0).T.astype(BF16)
    ctb = ct.astype(BF16)
    times_c = lambda m: lax.dot_general(m.astype(BF16), ctb, (((1,), (1,)), ((), ())),
                                        preferred_element_type=F32).astype(BF16)
    k0 = times_c(b)
    wt_ref[0:LANES, 0:LANES] = k0
    wt_ref[0:LANES, LANES:2 * LANES] = times_c(ab)
    wt_ref[LANES:2 * LANES, 0:LANES] = jnp.zeros((LANES, LANES), BF16)
    wt_ref[LANES:2 * LANES, LANES:2 * LANES] = k0


def _s5_chunk_maps(b, ct, a_re_lanes, a_im_lanes):
    spread = _spread_matrix()
    blk = lambda r, c: pl.BlockSpec((None, None, r, c), lambda l, k: (l, k, 0, 0))
    out = lambda r, c: jax.ShapeDtypeStruct((DEPTH, SLABS, r, c), BF16)
    return pl.pallas_call(
        _s5_chunk_kernel,
        grid=(DEPTH, SLABS),
        in_specs=[blk(LANES, 2 * STATE), blk(LANES, 2 * STATE), pl.BlockSpec(spread.shape, lambda l, k: (0, 0)),
                  blk(1, SLAB_W), blk(1, SLAB_W)],
        out_specs=(blk(2 * LANES, SLAB_W), blk(SLAB_W, 2 * LANES), blk(2 * LANES, 2 * LANES)),
        out_shape=(out(2 * LANES, SLAB_W), out(SLAB_W, 2 * LANES), out(2 * LANES, 2 * LANES)),
        name="s5_chunk_maps",
    )(b, ct, spread, a_re_lanes, a_im_lanes)


ADA_TILE = 1536


def _ada_kernel(c_ref, w_ref, b_ref, o_ref):
    c = c_ref[...]
    o_ref[...] = _bdot(c * _sigmoid(c), w_ref[...].astype(BF16)) + b_ref[...]


def _ada(c_all, w_ada, b_ada):
    rows = c_all.shape[0]
    return pl.pallas_call(
        _ada_kernel,
        grid=(DEPTH, 6 * D // ADA_TILE),
        in_specs=[
            pl.BlockSpec((rows, D), lambda l, n: (0, 0)),
            pl.BlockSpec((None, D, ADA_TILE), lambda l, n: (l, 0, n)),
            pl.BlockSpec((None, 1, ADA_TILE), lambda l, n: (l, 0, n)),
        ],
        out_specs=pl.BlockSpec((None, rows, ADA_TILE), lambda l, n: (l, 0, n)),
        out_shape=jax.ShapeDtypeStruct((DEPTH, rows, 6 * D), F32),
        name="adaln",
    )(c_all, w_ada, b_ada.reshape(DEPTH, 1, 6 * D))


def _mod_specs(l, rows, row_block, which):
    return [pl.BlockSpec((None, rows, D), lambda *_, n=n: (l, row_block, n)) for n in which]


def _mix_back(xb, h, y, mixed_fn, gt, win_ref, gv_ref, wglu_ref, bglu_ref, wout_ref):
    v = _rms(_gelu(_bdot(h, win_ref[:, D:2 * D])), gv_ref[...])
    y_a = _gelu(_bdot(h, win_ref[:, 0:D])) * mixed_fn(v)
    merged = _sigmoid(_bdot(h, win_ref[:, 3 * D:4 * D])) * y_a
    z = _gelu(y)
    y_b = z * _sigmoid(_bdot(z, wglu_ref[...]) + bglu_ref[...])
    merged = merged + _sigmoid(_bdot(h, win_ref[:, 4 * D:5 * D])) * y_b
    return xb + gt * _bdot(merged, wout_ref[...]), v


MIX_WEIGHTS = ('w_in', 'wb', 'wc', 'wt', 'w_glu', 'w_out')
W_S5_IN, W_S5_MAPS, W_REST = (0,), (1, 2, 3), (4, 5, 6, 7)


def _mix_weight_copies(l, hbm_refs, vmem_refs, sems):
    win_h, wb_h, wc_h, wt_h, wglu_h, wout_h = hbm_refs
    win_v, wb_v, wc_v, wt_v, wglu_v, wout_v = vmem_refs
    cols = lambda lo, hi: (win_h.at[l, :, lo * D:hi * D], win_v.at[:, lo * D:hi * D])
    pieces = [cols(2, 3), (wb_h.at[l], wb_v), (wt_h.at[l], wt_v), (wc_h.at[l], wc_v),
              cols(0, 2), cols(3, 5), (wglu_h.at[l], wglu_v), (wout_h.at[l], wout_v)]
    return [pltpu.make_async_copy(src, dst, sems.at[i]) for i, (src, dst) in enumerate(pieces)]


def _mix_weight_scratch(p):
    return [pltpu.VMEM(p[name].shape[1:], BF16) for name in MIX_WEIGHTS] + [pltpu.SemaphoreType.DMA((8,))]


def _pair_lanes(j):
    re = slice(j * PAIR_W, j * PAIR_W + LANES)
    im = slice(j * PAIR_W + LANES, (j + 1) * PAIR_W)
    half = slice(j * LANES, (j + 1) * LANES)
    return re, im, half


def _mix_prompt_kernel(x_ref, sh_ref, sc_ref, gt_ref, g1_ref, gv_ref, wsp_ref, bsp_ref,
                       ar_ref, ai_ref, dsk_ref, bglu_ref,
                       win_hbm, wb_hbm, wc_hbm, wt_hbm, wglu_hbm, wout_hbm,
                       xo_ref, hre_ref, him_ref, ps_scr, hs_scr, xc_scr, tril_scr,
                       win_ref, wb_ref, wc_ref, wt_ref, wglu_ref, wout_ref, w_sems, *, layer):
    nb = x_ref.shape[0]
    assert nb == SUBLANES
    step = pl.program_id(0)

    copies = _mix_weight_copies(layer, (win_hbm, wb_hbm, wc_hbm, wt_hbm, wglu_hbm, wout_hbm),
                                (win_ref, wb_ref, wc_ref, wt_ref, wglu_ref, wout_ref), w_sems)

    def await_weights(group):
        @pl.when(step == 0)
        def _():
            for i in group:
                copies[i].wait()

    @pl.when(step == 0)
    def _():
        for c in copies:
            c.start()
        hs_scr[0:SUBLANES, :] = jnp.zeros((SUBLANES, S_W), F32)

    row = lax.broadcasted_iota(jnp.int32, (CHUNK, CHUNK), 0)
    col = lax.broadcasted_iota(jnp.int32, (CHUNK, CHUNK), 1)
    for hd in range(HEADS):
        tril_scr[hd] = jnp.where(row >= col, wsp_ref[hd], 0.0).astype(BF16)

    def load_rows(c):
        xs, hs = [], []
        for bb in range(SEQ_PER_CHUNK):
            b = c * SEQ_PER_CHUNK + bb
            x = x_ref[b]
            xs.append(x)
            hs.append(_rms(x, g1_ref[...]) * (1.0 + sc_ref[pl.ds(b, 1), :]) + sh_ref[pl.ds(b, 1), :])
        return jnp.concatenate(xs, axis=0), jnp.concatenate(hs, axis=0).astype(BF16)

    def slab_rows(c, bb):
        return pl.ds(pl.multiple_of((c * SEQ_PER_CHUNK + bb) * PITCH, SUBLANES), SEQ_TILE)

    def phase_a(c, carry):
        _, h = load_rows(c)
        ps = _bdot(h, win_ref[:, 2 * D:3 * D])
        for bb in range(SEQ_PER_CHUNK):
            for k in range(SLABS):
                ps_scr[k, slab_rows(c, bb), :] = ps[bb * SEQ_TILE:(bb + 1) * SEQ_TILE, k * LANES:(k + 1) * LANES]
        return carry

    await_weights(W_S5_IN)
    lax.fori_loop(0, nb // SEQ_PER_CHUNK, phase_a, 0, unroll=True)

    def phase_b(sub, carry):
        t0 = sub * SUB_T

        def step_rows(k, i):
            return [pl.ds(t0 + CHUNK_T * c + i, SUBLANES, stride=PITCH) for c in range(SUB_C)]

        for k in range(SLABS):
            xc = jnp.concatenate(
                [jnp.concatenate([ps_scr[k, r, :] for r in step_rows(k, i)], axis=0) for i in range(CHUNK_T)],
                axis=1).astype(BF16)
            xc_scr[k] = xc
            hs_scr[SUBLANES:SUBLANES + SUB_ROWS, k * SLAB_W:(k + 1) * SLAB_W] = jnp.dot(
                xc, wb_ref[k], preferred_element_type=F32)
        for j in range(PAIRS):
            re, im, half = _pair_lanes(j)
            ar = jnp.broadcast_to(ar_ref[:, half], (SUBLANES, LANES))
            ai = jnp.broadcast_to(ai_ref[:, half], (SUBLANES, LANES))
            ar, ai = ar * ar - ai * ai, 2.0 * (ar * ai)
            hr = hs_scr[0:SUBLANES, re]
            hi = hs_scr[0:SUBLANES, im]
            for c in range(SUB_C):
                r = slice(SUBLANES * (c + 1), SUBLANES * (c + 2))
                hr, hi = (ar * hr - ai * hi + hs_scr[r, re], ar * hi + ai * hr + hs_scr[r, im])
                hs_scr[r, re] = hr
                hs_scr[r, im] = hi
        for k in range(SLABS):
            h_prev = hs_scr[0:SUB_ROWS, k * SLAB_W:(k + 1) * SLAB_W]
            y2 = (jnp.dot(xc_scr[k], wt_ref[k], preferred_element_type=F32)
                  + jnp.dot(h_prev.astype(BF16), wc_ref[k], preferred_element_type=F32))
            d_skip = jnp.broadcast_to(dsk_ref[:, k * LANES:(k + 1) * LANES], (SUBLANES, LANES))
            for i in range(CHUNK_T):
                for c, r in enumerate(step_rows(k, i)):
                    ps_scr[k, r, :] = (y2[c * SUBLANES:(c + 1) * SUBLANES, i * LANES:(i + 1) * LANES]
                                       + d_skip * ps_scr[k, r, :])
        hs_scr[0:SUBLANES, :] = hs_scr[SUB_ROWS:SUB_ROWS + SUBLANES, :]
        return carry

    await_weights(W_S5_MAPS)
    lax.fori_loop(0, SEQ_TILE // SUB_T, phase_b, 0, unroll=2)

    @pl.when(step == pl.num_programs(0) - 1)
    def _():
        for j in range(PAIRS):
            re, im, half = _pair_lanes(j)
            hre_ref[:, half] = hs_scr[0:SUBLANES, re]
            him_ref[:, half] = hs_scr[0:SUBLANES, im]

    def mixed_fn(v):
        vb = v.astype(BF16)
        rows = []
        for bb in range(SEQ_PER_CHUNK):
            heads = [jnp.dot(tril_scr[hd], vb[bb * CHUNK:(bb + 1) * CHUNK, hd * HEAD_DIM:(hd + 1) * HEAD_DIM],
                             preferred_element_type=F32) for hd in range(HEADS)]
            rows.append(jnp.concatenate(heads, axis=1) + bsp_ref[...])
        return jnp.concatenate(rows, axis=0)

    def phase_c(c, carry):
        xb, h = load_rows(c)
        y = jnp.concatenate(
            [jnp.concatenate([ps_scr[k, slab_rows(c, bb), :] for k in range(SLABS)], axis=1)
             for bb in range(SEQ_PER_CHUNK)], axis=0)
        gt = jnp.concatenate(
            [jnp.broadcast_to(gt_ref[pl.ds(c * SEQ_PER_CHUNK + bb, 1), :], (SEQ_TILE, D))
             for bb in range(SEQ_PER_CHUNK)], axis=0)
        xn, _ = _mix_back(xb, h, y, mixed_fn, gt, win_ref, gv_ref, wglu_ref, bglu_ref, wout_ref)
        for bb in range(SEQ_PER_CHUNK):
            xo_ref[c * SEQ_PER_CHUNK + bb] = xn[bb * SEQ_TILE:(bb + 1) * SEQ_TILE]
        return carry

    await_weights(W_REST)
    lax.fori_loop(0, nb // SEQ_PER_CHUNK, phase_c, 0, unroll=True)


def _mix_prompt(x, mod, mod_row_block, l, p):
    nb, seq, _ = x.shape
    assert nb == SUBLANES and seq % SEQ_TILE == 0
    tile = pl.BlockSpec((nb, SEQ_TILE, D), lambda s: (0, s, 0))
    consts = [p['g1'], p['g_v'], p['w_sp'], p['b_sp'], p['a_re'], p['a_im'], p['d_skip'], p['b_glu']]
    weights = [p[name] for name in MIX_WEIGHTS]
    state = jax.ShapeDtypeStruct((nb, S_HALF), F32)
    state_spec = pl.BlockSpec((nb, S_HALF), lambda s: (0, 0))
    return pl.pallas_call(
        functools.partial(_mix_prompt_kernel, layer=l),
        grid=(seq // SEQ_TILE,),
        in_specs=([tile] + _mod_specs(l, nb, mod_row_block, (0, 1, 2)) + [_layer_spec(a, l) for a in consts]
                  + [pl.BlockSpec(memory_space=pl.ANY)] * len(weights)),
        out_specs=(tile, state_spec, state_spec),
        out_shape=(jax.ShapeDtypeStruct(x.shape, F32), state, state),
        scratch_shapes=[
            pltpu.VMEM((SLABS, nb * PITCH, LANES), F32),
            pltpu.VMEM((SUB_ROWS + SUBLANES, S_W), F32),
            pltpu.VMEM((SLABS, SUB_ROWS, CHUNK_T * LANES), BF16),
            pltpu.VMEM((HEADS, CHUNK, CHUNK), BF16),
        ] + _mix_weight_scratch(p),
        compiler_params=pltpu.CompilerParams(
            dimension_semantics=("arbitrary",), vmem_limit_bytes=VMEM_LIMIT),
        name="mix_prompt",
    )(x, mod, mod, mod, *consts, *weights)


def _mix_sample_kernel(x_ref, sh_ref, sc_ref, gt_ref, g1_ref, gv_ref, wd_ref, bd_ref,
                       ar_ref, ai_ref, dsk_ref, bglu_ref, h0r_ref, h0i_ref,
                       win_hbm, wb_hbm, wc_hbm, wt_hbm, wglu_hbm, wout_hbm,
                       xo_ref, hre_ref, him_ref, v_ref,
                       win_ref, wb_ref, wc_ref, wt_ref, wglu_ref, wout_ref, w_sems, *, layer):
    copies = _mix_weight_copies(layer, (win_hbm, wb_hbm, wc_hbm, wt_hbm, wglu_hbm, wout_hbm),
                                (win_ref, wb_ref, wc_ref, wt_ref, wglu_ref, wout_ref), w_sems)
    for c in copies:
        c.start()
    xb = x_ref[...]
    h = (_rms(xb, g1_ref[...]) * (1.0 + sc_ref[...]) + sh_ref[...]).astype(BF16)
    for i in W_S5_IN:
        copies[i].wait()
    ps = _bdot(h, win_ref[:, 2 * D:3 * D])
    for i in W_S5_MAPS:
        copies[i].wait()
    ys = []
    for k in range(SLABS):
        xk = ps[:, k * LANES:(k + 1) * LANES].astype(BF16)
        bu = jnp.dot(xk, wb_ref[k, LANES:2 * LANES, :], preferred_element_type=F32)
        h0 = []
        for q in range(SLAB_PAIRS):
            re, im, half = _pair_lanes(q)
            _, _, half = _pair_lanes(k * SLAB_PAIRS + q)
            ar, ai = ar_ref[:, half], ai_ref[:, half]
            hr, hi = h0r_ref[half, :].T, h0i_ref[half, :].T
            hre_ref[half, :] = (ar * hr - ai * hi + bu[:, re]).T
            him_ref[half, :] = (ar * hi + ai * hr + bu[:, im]).T
            h0 += [hr, hi]
        ys.append(_bdot(jnp.concatenate(h0, axis=1), wc_ref[k, :, 0:LANES])
                  + jnp.dot(xk, wt_ref[k, 0:LANES, 0:LANES], preferred_element_type=F32))
    mixed_fn = lambda v: wd_ref[...] * v + bd_ref[...]
    y = jnp.concatenate(ys, axis=1) + dsk_ref[...] * ps
    for i in W_REST:
        copies[i].wait()
    xn, v = _mix_back(xb, h, y, mixed_fn, gt_ref[...], win_ref, gv_ref, wglu_ref, bglu_ref, wout_ref)
    xo_ref[...] = xn
    v_ref[...] = v


def _mix_sample(x, mod, l, h0_re, h0_im, p):
    rows = x.shape[0]
    assert rows == LANES
    consts = [p['g1'], p['g_v'], p['w_diag'], p['b_diag'], p['a_re'], p['a_im'], p['d_skip'], p['b_glu'],
              h0_re, h0_im]
    weights = [p[name] for name in MIX_WEIGHTS]
    whole = lambda shape: pl.BlockSpec(shape, lambda i: (0,) * len(shape))
    state = jax.ShapeDtypeStruct((S_HALF, rows), F32)
    return pl.pallas_call(
        functools.partial(_mix_sample_kernel, layer=l),
        grid=(1,),
        in_specs=([whole((rows, D))] + _mod_specs(l, rows, 0, (0, 1, 2)) + [_layer_spec(a, l) for a in consts]
                  + [pl.BlockSpec(memory_space=pl.ANY)] * len(weights)),
        out_specs=(whole((rows, D)), whole((S_HALF, rows)), whole((S_HALF, rows)), whole((rows, D))),
        out_shape=(jax.ShapeDtypeStruct((rows, D), F32), state, state, jax.ShapeDtypeStruct((rows, D), F32)),
        scratch_shapes=_mix_weight_scratch(p),
        compiler_params=pltpu.CompilerParams(
            dimension_semantics=("arbitrary",), vmem_limit_bytes=VMEM_LIMIT),
        name="mix_sample",
    )(x, mod, mod, mod, *consts, *weights)


def _mlp_kernel(x_ref, sh_ref, sc_ref, gt_ref, g2_ref, w1_ref, w2_ref, gf_ref, xo_ref, *, final, tiles_per_mod):
    x = x_ref[...]
    if tiles_per_mod:
        b = pl.program_id(0) // tiles_per_mod
        sh, sc, gt = (r[pl.ds(b, 1), :] for r in (sh_ref, sc_ref, gt_ref))
    else:
        sh, sc, gt = sh_ref[...], sc_ref[...], gt_ref[...]
    h = (_rms(x, g2_ref[...]) * (1.0 + sc) + sh).astype(BF16)
    acc = jnp.zeros(x.shape, F32)
    for k in range(D_FF // D):
        a = jnp.dot(h, w1_ref[:, k * D:(k + 1) * D], preferred_element_type=F32)
        acc = acc + _bdot(jnp.square(jnp.maximum(a, 0.0)), w2_ref[k * D:(k + 1) * D, :])
    xn = x + gt * acc
    if final:
        xn = _rms(xn, gf_ref[...])
    xo_ref[...] = xn


def _mlp(x, mod, mod_row_block, l, p, g_final, final):
    if x.ndim == 3:
        nb, seq, _ = x.shape
        tm = min(MLP_ROWS, seq)
        assert seq % tm == 0
        per = seq // tm
        grid = (nb * per,)
        tile = pl.BlockSpec((None, tm, D), lambda i: (i // per, i % per, 0))
        mod_rows = nb
    else:
        per = 0
        grid = (1,)
        tile = pl.BlockSpec(x.shape, lambda i: (0, 0))
        mod_rows = x.shape[0]
    return pl.pallas_call(
        functools.partial(_mlp_kernel, final=final, tiles_per_mod=per),
        grid=grid,
        in_specs=[tile] + _mod_specs(l, mod_rows, mod_row_block, (3, 4, 5)) + [
            _layer_spec(p['g2'], l), _layer_spec(p['w_ff1'], l), _layer_spec(p['w_ff2'], l),
            pl.BlockSpec((1, D), lambda i: (0, 0))],
        out_specs=tile,
        out_shape=jax.ShapeDtypeStruct(x.shape, F32),
        compiler_params=pltpu.CompilerParams(
            dimension_semantics=("arbitrary",), vmem_limit_bytes=VMEM_LIMIT),
        name="mlp_final" if final else "mlp",
    )(x, mod, mod, mod, p['g2'], p['w_ff1'], p['w_ff2'], g_final)


def kernel(x_prompt, x_sample, c_prompt, c_sample, state_ssm_re, state_ssm_im, w_ada, b_ada, g_norm1, g_norm2, w_in, g_v, w_spatial, b_spatial, lam_re, lam_im, log_dt, b_re, b_im, c_re, c_im, d_skip, w_glu, b_glu, w_out, w_ff1, w_ff2, g_final):
    nbp, seq, _ = x_prompt.shape
    nbs = x_sample.shape[0]
    assert x_sample.shape[1] == 1 and nbs % nbp == 0

    a_re, a_im, bb_re, bb_im = _s5_prep(lam_re, lam_im, log_dt, b_re, b_im)
    wb, wc, wt = _s5_chunk_maps(*_s5_slab_layout(a_re, a_im, bb_re, bb_im, c_re, c_im))
    mod = _ada(jnp.concatenate([c_sample, c_prompt], axis=0), w_ada, b_ada)
    prompt_row_block = nbs // nbp

    vec = lambda a: a.reshape(DEPTH, 1, D)
    p = dict(
        g1=vec(g_norm1), g2=vec(g_norm2), g_v=vec(g_v), d_skip=vec(d_skip), b_glu=vec(b_glu),
        w_in=w_in.astype(BF16), w_glu=w_glu.astype(BF16), w_out=w_out.astype(BF16),
        w_ff1=w_ff1.astype(BF16), w_ff2=w_ff2.astype(BF16),
        w_sp=w_spatial, b_sp=jnp.repeat(jnp.swapaxes(b_spatial, 1, 2), HEAD_DIM, axis=2),
        w_diag=vec(jnp.repeat(w_spatial[:, :, 0, 0], HEAD_DIM, axis=1)),
        b_diag=vec(jnp.repeat(b_spatial[:, :, 0], HEAD_DIM, axis=1)),
        a_re=a_re.reshape(DEPTH, 1, S_HALF), a_im=a_im.reshape(DEPTH, 1, S_HALF), wb=wb, wc=wc, wt=wt)
    gf = g_final.reshape(1, D)
    lane_major = lambda s: jnp.transpose(s, (0, 2, 3, 1)).reshape(DEPTH, S_HALF, nbs)
    h0_re, h0_im = lane_major(state_ssm_re), lane_major(state_ssm_im)

    xp = x_prompt
    xs = x_sample.reshape(nbs, D)
    re_p, im_p, re_s, im_s, v_s = [], [], [], [], []
    for l in range(DEPTH):
        final = l == DEPTH - 1
        xp, hr, hi = _mix_prompt(xp, mod, prompt_row_block, l, p)
        xp = _mlp(xp, mod, prompt_row_block, l, p, gf, final)
        re_p.append(hr)
        im_p.append(hi)
        xs, hr, hi, v = _mix_sample(xs, mod, l, h0_re, h0_im, p)
        xs = _mlp(xs, mod, 0, l, p, gf, final)
        re_s.append(hr)
        im_s.append(hi)
        v_s.append(v)

    state = lambda hs: jnp.stack(hs).reshape(DEPTH, nbp, GROUPS, STATE)
    state_t = lambda hs: jnp.transpose(jnp.stack(hs).reshape(DEPTH, GROUPS, STATE, nbs), (0, 3, 1, 2))
    return (xp, xs.reshape(nbs, 1, D), state(re_p), state(im_p),
            state_t(re_s), state_t(im_s), jnp.stack(v_s).reshape(DEPTH, nbs, 1, D))
```

```python
import functools
import math

import jax
import jax.numpy as jnp
import numpy as np
from jax import lax
from jax.experimental import pallas as pl
from jax.experimental.pallas import tpu as pltpu

F32 = jnp.float32
BF16 = jnp.bfloat16

D = 1024
DEPTH = 2
CHUNK = 128
HEADS = 4
HEAD_DIM = D // HEADS
GROUPS = 64
GROUP_W = 16
STATE = 64
D_FF = 4 * D
N_IN = 5 * D
EPS = 1e-6

LANES = 128
SUBLANES = 8
PAIRS = GROUPS // 2
PAIR_W = 4 * STATE
S_W = PAIRS * PAIR_W
S_HALF = S_W // 2
SLABS = D // LANES
SLAB_PAIRS = LANES // (2 * GROUP_W)
SLAB_W = SLAB_PAIRS * PAIR_W
CHUNK_T = 2

SEQ_TILE = CHUNK
ROW_CHUNK = 256
SEQ_PER_CHUNK = ROW_CHUNK // SEQ_TILE
SUB_C = 32
SUB_ROWS = SUB_C * SUBLANES
SUB_T = SUB_C * CHUNK_T
PITCH = SEQ_TILE + SUBLANES
MLP_ROWS = 1024
VMEM_LIMIT = 61 * 1024 * 1024

_GELU_C = math.sqrt(2.0 / math.pi)
_LOG2E = math.log2(math.e)


def _gelu(x):
    t = jnp.tanh(x * (_GELU_C + (_GELU_C * 0.044715) * (x * x)))
    hx = 0.5 * x
    return hx + hx * t


def _sigmoid(x):
    return 1.0 / (1.0 + jnp.exp2(x * (-_LOG2E)))


def _rms(x, g):
    return x * lax.rsqrt(jnp.mean(x * x, axis=-1, keepdims=True) + EPS) * g


def _bdot(a, b):
    return jnp.dot(a.astype(BF16), b, preferred_element_type=F32)


def _layer_spec(a, l):
    zeros = (0,) * (a.ndim - 1)
    return pl.BlockSpec((None,) + a.shape[1:], lambda *_: (l,) + zeros, pipeline_mode=pl.Buffered(1))


def _s5_prep_kernel(lr_ref, li_ref, ldt_ref, lrw_ref, liw_ref, ldtw_ref, br_ref, bi_ref,
                    ar_ref, ai_ref, bbr_ref, bbi_ref):
    def a_bar(lr, li, ldt):
        dt = jnp.exp(ldt)
        mag = jnp.exp(lr * dt)
        return mag * jnp.cos(li * dt), mag * jnp.sin(li * dt)

    ar, ai = a_bar(lr_ref[...], li_ref[...], ldt_ref[...])
    ar_ref[...] = ar
    ai_ref[...] = ai
    lr, li = lrw_ref[...], liw_ref[...]
    ar, ai = a_bar(lr, li, ldtw_ref[...])
    nr, ni = ar - 1.0, ai
    den = lr * lr + li * li
    cr = (nr * lr + ni * li) / den
    ci = (ni * lr - nr * li) / den
    br, bi = br_ref[...], bi_ref[...]
    bbr_ref[...] = cr * br - ci * bi
    bbi_ref[...] = cr * bi + ci * br


def _s5_prep(lam_re, lam_im, log_dt, b_re, b_im):
    wide = (DEPTH, GROUPS, STATE * GROUP_W)
    ldt = jnp.broadcast_to(log_dt[:, :, None], lam_re.shape)
    lam_w = jnp.repeat(jnp.stack([lam_re, lam_im, ldt]), GROUP_W, axis=-1)
    small = jax.ShapeDtypeStruct(lam_re.shape, F32)
    big = jax.ShapeDtypeStruct(wide, F32)
    return pl.pallas_call(
        _s5_prep_kernel, out_shape=(small, small, big, big), name="s5_prep",
    )(lam_re, lam_im, ldt, lam_w[0], lam_w[1], lam_w[2], b_re.reshape(wide), b_im.reshape(wide))


def _s5_slab_layout(a_re, a_im, bb_re, bb_im, c_re, c_im):
    bb = jnp.stack([bb_re, bb_im]).reshape(2, DEPTH, SLABS, SLAB_PAIRS, 2, STATE, GROUP_W)
    b = jnp.transpose(bb, (1, 2, 3, 4, 6, 0, 5)).reshape(DEPTH, SLABS, LANES, 2 * STATE)
    cc = jnp.stack([c_re, -c_im]).reshape(2, DEPTH, SLABS, LANES, STATE)
    ct = jnp.transpose(cc, (1, 2, 3, 0, 4)).reshape(DEPTH, SLABS, LANES, 2 * STATE)

    def lanes(a):
        a = a.reshape(DEPTH, SLABS, SLAB_PAIRS, 1, 2 * STATE)
        return jnp.broadcast_to(a, (DEPTH, SLABS, SLAB_PAIRS, 2, 2 * STATE)).reshape(DEPTH, SLABS, 1, SLAB_W)

    return b, ct, lanes(a_re), lanes(a_im)


def _spread_matrix():
    spread = np.zeros((2, STATE, SLAB_PAIRS, 2, 2, STATE), np.float32)
    for r in range(2):
        spread[r, np.arange(STATE), :, r, :, np.arange(STATE)] = 1.0
    return jnp.asarray(spread.reshape(2 * STATE, SLAB_W), BF16)


def _cmul_cols(m, ar_row, ai_row, sign):
    parts = []
    for q in range(SLAB_PAIRS):
        re = slice(q * PAIR_W, q * PAIR_W + LANES)
        im = slice(q * PAIR_W + LANES, (q + 1) * PAIR_W)
        ar, ai = ar_row[:, re], sign * ai_row[:, re]
        parts += [m[:, re] * ar - m[:, im] * ai, m[:, im] * ar + m[:, re] * ai]
    return jnp.concatenate(parts, axis=1)


def _s5_chunk_kernel(b_ref, ct_ref, spread_ref, ar_ref, ai_ref, wb_ref, wc_ref, wt_ref):
    lane = lax.broadcasted_iota(jnp.int32, (LANES, SLAB_W), 1)
    chan = lax.broadcasted_iota(jnp.int32, (LANES, SLAB_W), 0)
    same_group = (lane // PAIR_W) * 2 + (lane // STATE) % 2 == chan // GROUP_W
    b = jnp.where(same_group, _bdot(b_ref[...], spread_ref[...]), 0.0)
    ct = jnp.where(same_group, _bdot(ct_ref[...], spread_ref[...]), 0.0)
    ar, ai = ar_ref[...], ai_ref[...]
    ab = _cmul_cols(b, ar, ai, 1.0)
    wb_ref[0:LANES, :] = ab.astype(BF16)
    wb_ref[LANES:2 * LANES, :] = b.astype(BF16)
    cat = _cmul_cols(ct, ar, ai, -1.0)
    wc_ref[:, 0:LANES] = cat.T.astype(BF16)
    wc_ref[:, LANES:2 * LANES] = _cmul_cols(cat, ar, ai, -1.0).T.astype(BF16)
    ctb = ct.astype(BF16)
    times_c = lambda m: lax.dot_general(m.astype(BF16), ctb, (((1,), (1,)), ((), ())),
                                        preferred_element_type=F32).astype(BF16)
    k0 = times_c(b)
    wt_ref[0:LANES, 0:LANES] = k0
    wt_ref[0:LANES, LANES:2 * LANES] = times_c(ab)
    wt_ref[LANES:2 * LANES, 0:LANES] = jnp.zeros((LANES, LANES), BF16)
    wt_ref[LANES:2 * LANES, LANES:2 * LANES] = k0


def _s5_chunk_maps(b, ct, a_re_lanes, a_im_lanes):
    spread = _spread_matrix()
    blk = lambda r, c: pl.BlockSpec((None, None, r, c), lambda l, k: (l, k, 0, 0))
    out = lambda r, c: jax.ShapeDtypeStruct((DEPTH, SLABS, r, c), BF16)
    return pl.pallas_call(
        _s5_chunk_kernel,
        grid=(DEPTH, SLABS),
        in_specs=[blk(LANES, 2 * STATE), blk(LANES, 2 * STATE), pl.BlockSpec(spread.shape, lambda l, k: (0, 0)),
                  blk(1, SLAB_W), blk(1, SLAB_W)],
        out_specs=(blk(2 * LANES, SLAB_W), blk(SLAB_W, 2 * LANES), blk(2 * LANES, 2 * LANES)),
        out_shape=(out(2 * LANES, SLAB_W), out(SLAB_W, 2 * LANES), out(2 * LANES, 2 * LANES)),
        name="s5_chunk_maps",
    )(b, ct, spread, a_re_lanes, a_im_lanes)


ADA_TILE = 1536


def _ada_kernel(c_ref, w_ref, b_ref, o_ref):
    c = c_ref[...]
    o_ref[...] = _bdot(c * _sigmoid(c), w_ref[...].astype(BF16)) + b_ref[...]


def _ada(c_all, w_ada, b_ada):
    rows = c_all.shape[0]
    return pl.pallas_call(
        _ada_kernel,
        grid=(DEPTH, 6 * D // ADA_TILE),
        in_specs=[
            pl.BlockSpec((rows, D), lambda l, n: (0, 0)),
            pl.BlockSpec((None, D, ADA_TILE), lambda l, n: (l, 0, n)),
            pl.BlockSpec((None, 1, ADA_TILE), lambda l, n: (l, 0, n)),
        ],
        out_specs=pl.BlockSpec((None, rows, ADA_TILE), lambda l, n: (l, 0, n)),
        out_shape=jax.ShapeDtypeStruct((DEPTH, rows, 6 * D), F32),
        name="adaln",
    )(c_all, w_ada, b_ada.reshape(DEPTH, 1, 6 * D))


def _mod_specs(l, rows, row_block, which):
    return [pl.BlockSpec((None, rows, D), lambda *_, n=n: (l, row_block, n)) for n in which]


def _mix_back(xb, h, y, mixed_fn, gt, win_ref, gv_ref, wglu_ref, bglu_ref, wout_ref):
    col = lambda n, hd: slice(n * D + hd * HEAD_DIM, n * D + (hd + 1) * HEAD_DIM)
    g = [_gelu(_bdot(h, win_ref[:, col(1, hd)])) for hd in range(HEADS)]
    ss = g[0] * g[0]
    for hd in range(1, HEADS):
        ss = ss + g[hd] * g[hd]
    scale = lax.rsqrt(jnp.sum(ss, axis=-1, keepdims=True) * (1.0 / D) + EPS)
    v = jnp.concatenate([g[hd] * scale for hd in range(HEADS)], axis=1) * gv_ref[...]
    mixed = mixed_fn(v)
    z = _gelu(y)
    zb = z.astype(BF16)
    parts = []
    for hd in range(HEADS):
        lanes = slice(hd * HEAD_DIM, (hd + 1) * HEAD_DIM)
        y_a = _gelu(_bdot(h, win_ref[:, col(0, hd)])) * mixed[:, lanes]
        m = _sigmoid(_bdot(h, win_ref[:, col(3, hd)])) * y_a
        y_b = z[:, lanes] * _sigmoid(jnp.dot(zb, wglu_ref[:, lanes], preferred_element_type=F32) + bglu_ref[:, lanes])
        m = m + _sigmoid(_bdot(h, win_ref[:, col(4, hd)])) * y_b
        parts.append(m.astype(BF16))
    merged = jnp.concatenate(parts, axis=1)
    return xb + gt * jnp.dot(merged, wout_ref[...], preferred_element_type=F32), v


MIX_WEIGHTS = ('w_in', 'wb', 'wc', 'wt', 'w_glu', 'w_out')
W_S5_IN, W_S5_MAPS, W_REST = (0,), (1, 2, 3), (4, 5, 6, 7)


def _mix_weight_copies(l, hbm_refs, vmem_refs, sems):
    win_h, wb_h, wc_h, wt_h, wglu_h, wout_h = hbm_refs
    win_v, wb_v, wc_v, wt_v, wglu_v, wout_v = vmem_refs
    cols = lambda lo, hi: (win_h.at[l, :, lo * D:hi * D], win_v.at[:, lo * D:hi * D])
    pieces = [cols(2, 3), (wb_h.at[l], wb_v), (wt_h.at[l], wt_v), (wc_h.at[l], wc_v),
              cols(0, 2), cols(3, 5), (wglu_h.at[l], wglu_v), (wout_h.at[l], wout_v)]
    return [pltpu.make_async_copy(src, dst, sems.at[i]) for i, (src, dst) in enumerate(pieces)]


def _mix_weight_scratch(p):
    return [pltpu.VMEM(p[name].shape[1:], BF16) for name in MIX_WEIGHTS] + [pltpu.SemaphoreType.DMA((8,))]


def _pair_lanes(j):
    re = slice(j * PAIR_W, j * PAIR_W + LANES)
    im = slice(j * PAIR_W + LANES, (j + 1) * PAIR_W)
    half = slice(j * LANES, (j + 1) * LANES)
    return re, im, half


def _mix_prompt_kernel(x_ref, sh_ref, sc_ref, gt_ref, g1_ref, gv_ref, wsp_ref, bsp_ref,
                       ar_ref, ai_ref, dsk_ref, bglu_ref,
                       win_hbm, wb_hbm, wc_hbm, wt_hbm, wglu_hbm, wout_hbm,
                       xo_ref, hre_ref, him_ref, ps_scr, hs_scr, xc_scr, tril_scr,
                       win_ref, wb_ref, wc_ref, wt_ref, wglu_ref, wout_ref, w_sems, *, layer):
    nb = x_ref.shape[0]
    assert nb == SUBLANES
    step = pl.program_id(0)

    copies = _mix_weight_copies(layer, (win_hbm, wb_hbm, wc_hbm, wt_hbm, wglu_hbm, wout_hbm),
                                (win_ref, wb_ref, wc_ref, wt_ref, wglu_ref, wout_ref), w_sems)

    def await_weights(group):
        @pl.when(step == 0)
        def _():
            for i in group:
                copies[i].wait()

    @pl.when(step == 0)
    def _():
        for c in copies:
            c.start()
        hs_scr[0:SUBLANES, :] = jnp.zeros((SUBLANES, S_W), F32)

    row = lax.broadcasted_iota(jnp.int32, (CHUNK, CHUNK), 0)
    col = lax.broadcasted_iota(jnp.int32, (CHUNK, CHUNK), 1)
    for hd in range(HEADS):
        tril_scr[hd] = jnp.where(row >= col, wsp_ref[hd], 0.0).astype(BF16)

    def load_rows(c):
        xs, hs = [], []
        for bb in range(SEQ_PER_CHUNK):
            b = c * SEQ_PER_CHUNK + bb
            x = x_ref[b]
            xs.append(x)
            hs.append(_rms(x, g1_ref[...]) * (1.0 + sc_ref[pl.ds(b, 1), :]) + sh_ref[pl.ds(b, 1), :])
        return jnp.concatenate(xs, axis=0), jnp.concatenate(hs, axis=0).astype(BF16)

    def slab_rows(c, bb):
        return pl.ds(pl.multiple_of((c * SEQ_PER_CHUNK + bb) * PITCH, SUBLANES), SEQ_TILE)

    def phase_a(c, carry):
        _, h = load_rows(c)
        ps = _bdot(h, win_ref[:, 2 * D:3 * D])
        for bb in range(SEQ_PER_CHUNK):
            for k in range(SLABS):
                ps_scr[k, slab_rows(c, bb), :] = ps[bb * SEQ_TILE:(bb + 1) * SEQ_TILE, k * LANES:(k + 1) * LANES]
        return carry

    await_weights(W_S5_IN)
    lax.fori_loop(0, nb // SEQ_PER_CHUNK, phase_a, 0, unroll=True)

    def phase_b(sub, carry):
        t0 = sub * SUB_T

        def step_rows(k, i):
            return [pl.ds(t0 + CHUNK_T * c + i, SUBLANES, stride=PITCH) for c in range(SUB_C)]

        for k in range(SLABS):
            xc = jnp.concatenate(
                [jnp.concatenate([ps_scr[k, r, :] for r in step_rows(k, i)], axis=0) for i in range(CHUNK_T)],
                axis=1).astype(BF16)
            xc_scr[k] = xc
            hs_scr[SUBLANES:SUBLANES + SUB_ROWS, k * SLAB_W:(k + 1) * SLAB_W] = jnp.dot(
                xc, wb_ref[k], preferred_element_type=F32)
        for j in range(PAIRS):
            re, im, half = _pair_lanes(j)
            ar = jnp.broadcast_to(ar_ref[:, half], (SUBLANES, LANES))
            ai = jnp.broadcast_to(ai_ref[:, half], (SUBLANES, LANES))
            ar, ai = ar * ar - ai * ai, 2.0 * (ar * ai)
            hr = hs_scr[0:SUBLANES, re]
            hi = hs_scr[0:SUBLANES, im]
            for c in range(SUB_C):
                r = slice(SUBLANES * (c + 1), SUBLANES * (c + 2))
                hr, hi = (ar * hr - ai * hi + hs_scr[r, re], ar * hi + ai * hr + hs_scr[r, im])
                hs_scr[r, re] = hr
                hs_scr[r, im] = hi
        for k in range(SLABS):
            h_prev = hs_scr[0:SUB_ROWS, k * SLAB_W:(k + 1) * SLAB_W]
            y2 = (jnp.dot(xc_scr[k], wt_ref[k], preferred_element_type=F32)
                  + jnp.dot(h_prev.astype(BF16), wc_ref[k], preferred_element_type=F32))
            d_skip = jnp.broadcast_to(dsk_ref[:, k * LANES:(k + 1) * LANES], (SUBLANES, LANES))
            for i in range(CHUNK_T):
                for c, r in enumerate(step_rows(k, i)):
                    ps_scr[k, r, :] = (y2[c * SUBLANES:(c + 1) * SUBLANES, i * LANES:(i + 1) * LANES]
                                       + d_skip * ps_scr[k, r, :])
        hs_scr[0:SUBLANES, :] = hs_scr[SUB_ROWS:SUB_ROWS + SUBLANES, :]
        return carry

    await_weights(W_S5_MAPS)
    lax.fori_loop(0, SEQ_TILE // SUB_T, phase_b, 0, unroll=2)

    @pl.when(step == pl.num_programs(0) - 1)
    def _():
        for j in range(PAIRS):
            re, im, half = _pair_lanes(j)
            hre_ref[:, half] = hs_scr[0:SUBLANES, re]
            him_ref[:, half] = hs_scr[0:SUBLANES, im]

    def mixed_fn(v):
        vb = v.astype(BF16)
        rows = []
        for bb in range(SEQ_PER_CHUNK):
            heads = [jnp.dot(tril_scr[hd], vb[bb * CHUNK:(bb + 1) * CHUNK, hd * HEAD_DIM:(hd + 1) * HEAD_DIM],
                             preferred_element_type=F32) for hd in range(HEADS)]
            rows.append(jnp.concatenate(heads, axis=1) + bsp_ref[...])
        return jnp.concatenate(rows, axis=0)

    def phase_c(c, carry):
        xb, h = load_rows(c)
        y = jnp.concatenate(
            [jnp.concatenate([ps_scr[k, slab_rows(c, bb), :] for k in range(SLABS)], axis=1)
             for bb in range(SEQ_PER_CHUNK)], axis=0)
        gt = jnp.concatenate(
            [jnp.broadcast_to(gt_ref[pl.ds(c * SEQ_PER_CHUNK + bb, 1), :], (SEQ_TILE, D))
             for bb in range(SEQ_PER_CHUNK)], axis=0)
        xn, _ = _mix_back(xb, h, y, mixed_fn, gt, win_ref, gv_ref, wglu_ref, bglu_ref, wout_ref)
        for bb in range(SEQ_PER_CHUNK):
            xo_ref[c * SEQ_PER_CHUNK + bb] = xn[bb * SEQ_TILE:(bb + 1) * SEQ_TILE]
        return carry

    await_weights(W_REST)
    lax.fori_loop(0, nb // SEQ_PER_CHUNK, phase_c, 0, unroll=2)


def _mix_prompt(x, mod, mod_row_block, l, p):
    nb, seq, _ = x.shape
    assert nb == SUBLANES and seq % SEQ_TILE == 0
    tile = pl.BlockSpec((nb, SEQ_TILE, D), lambda s: (0, s, 0))
    consts = [p['g1'], p['g_v'], p['w_sp'], p['b_sp'], p['a_re'], p['a_im'], p['d_skip'], p['b_glu']]
    weights = [p[name] for name in MIX_WEIGHTS]
    state = jax.ShapeDtypeStruct((nb, S_HALF), F32)
    state_spec = pl.BlockSpec((nb, S_HALF), lambda s: (0, 0))
    return pl.pallas_call(
        functools.partial(_mix_prompt_kernel, layer=l),
        grid=(seq // SEQ_TILE,),
        in_specs=([tile] + _mod_specs(l, nb, mod_row_block, (0, 1, 2)) + [_layer_spec(a, l) for a in consts]
                  + [pl.BlockSpec(memory_space=pl.ANY)] * len(weights)),
        out_specs=(tile, state_spec, state_spec),
        out_shape=(jax.ShapeDtypeStruct(x.shape, F32), state, state),
        scratch_shapes=[
            pltpu.VMEM((SLABS, nb * PITCH, LANES), F32),
            pltpu.VMEM((SUB_ROWS + SUBLANES, S_W), F32),
            pltpu.VMEM((SLABS, SUB_ROWS, CHUNK_T * LANES), BF16),
            pltpu.VMEM((HEADS, CHUNK, CHUNK), BF16),
        ] + _mix_weight_scratch(p),
        compiler_params=pltpu.CompilerParams(
            dimension_semantics=("arbitrary",), vmem_limit_bytes=VMEM_LIMIT),
        name="mix_prompt",
    )(x, mod, mod, mod, *consts, *weights)


def _mix_sample_kernel(x_ref, sh_ref, sc_ref, gt_ref, g1_ref, gv_ref, wd_ref, bd_ref,
                       ar_ref, ai_ref, dsk_ref, bglu_ref, h0r_ref, h0i_ref,
                       win_hbm, wb_hbm, wc_hbm, wt_hbm, wglu_hbm, wout_hbm,
                       xo_ref, hre_ref, him_ref, v_ref,
                       win_ref, wb_ref, wc_ref, wt_ref, wglu_ref, wout_ref, w_sems, *, layer):
    copies = _mix_weight_copies(layer, (win_hbm, wb_hbm, wc_hbm, wt_hbm, wglu_hbm, wout_hbm),
                                (win_ref, wb_ref, wc_ref, wt_ref, wglu_ref, wout_ref), w_sems)
    for c in copies:
        c.start()
    xb = x_ref[...]
    h = (_rms(xb, g1_ref[...]) * (1.0 + sc_ref[...]) + sh_ref[...]).astype(BF16)
    for i in W_S5_IN:
        copies[i].wait()
    ps = _bdot(h, win_ref[:, 2 * D:3 * D])
    for i in W_S5_MAPS:
        copies[i].wait()
    ys = []
    for k in range(SLABS):
        xk = ps[:, k * LANES:(k + 1) * LANES].astype(BF16)
        bu = jnp.dot(xk, wb_ref[k, LANES:2 * LANES, :], preferred_element_type=F32)
        h0 = []
        for q in range(SLAB_PAIRS):
            re, im, half = _pair_lanes(q)
            _, _, half = _pair_lanes(k * SLAB_PAIRS + q)
            ar, ai = ar_ref[:, half], ai_ref[:, half]
            hr, hi = h0r_ref[half, :].T, h0i_ref[half, :].T
            hre_ref[half, :] = (ar * hr - ai * hi + bu[:, re]).T
            him_ref[half, :] = (ar * hi + ai * hr + bu[:, im]).T
            h0 += [hr, hi]
        ys.append(_bdot(jnp.concatenate(h0, axis=1), wc_ref[k, :, 0:LANES])
                  + jnp.dot(xk, wt_ref[k, 0:LANES, 0:LANES], preferred_element_type=F32))
    mixed_fn = lambda v: wd_ref[...] * v + bd_ref[...]
    y = jnp.concatenate(ys, axis=1) + dsk_ref[...] * ps
    for i in W_REST:
        copies[i].wait()
    xn, v = _mix_back(xb, h, y, mixed_fn, gt_ref[...], win_ref, gv_ref, wglu_ref, bglu_ref, wout_ref)
    xo_ref[...] = xn
    v_ref[...] = v


def _mix_sample(x, mod, l, h0_re, h0_im, p):
    rows = x.shape[0]
    assert rows == LANES
    consts = [p['g1'], p['g_v'], p['w_diag'], p['b_diag'], p['a_re'], p['a_im'], p['d_skip'], p['b_glu'],
              h0_re, h0_im]
    weights = [p[name] for name in MIX_WEIGHTS]
    whole = lambda shape: pl.BlockSpec(shape, lambda i: (0,) * len(shape))
    state = jax.ShapeDtypeStruct((S_HALF, rows), F32)
    return pl.pallas_call(
        functools.partial(_mix_sample_kernel, layer=l),
        grid=(1,),
        in_specs=([whole((rows, D))] + _mod_specs(l, rows, 0, (0, 1, 2)) + [_layer_spec(a, l) for a in consts]
                  + [pl.BlockSpec(memory_space=pl.ANY)] * len(weights)),
        out_specs=(whole((rows, D)), whole((S_HALF, rows)), whole((S_HALF, rows)), whole((rows, D))),
        out_shape=(jax.ShapeDtypeStruct((rows, D), F32), state, state, jax.ShapeDtypeStruct((rows, D), F32)),
        scratch_shapes=_mix_weight_scratch(p),
        compiler_params=pltpu.CompilerParams(
            dimension_semantics=("arbitrary",), vmem_limit_bytes=VMEM_LIMIT),
        name="mix_sample",
    )(x, mod, mod, mod, *consts, *weights)


def _mlp_kernel(x_ref, sh_ref, sc_ref, gt_ref, g2_ref, w1_ref, w2_ref, gf_ref, xo_ref, *, final, tiles_per_mod):
    x = x_ref[...]
    if tiles_per_mod:
        b = pl.program_id(0) // tiles_per_mod
        sh, sc, gt = (r[pl.ds(b, 1), :] for r in (sh_ref, sc_ref, gt_ref))
    else:
        sh, sc, gt = sh_ref[...], sc_ref[...], gt_ref[...]
    h = (_rms(x, g2_ref[...]) * (1.0 + sc) + sh).astype(BF16)
    acc = jnp.zeros(x.shape, F32)
    for k in range(D_FF // D):
        a = jnp.dot(h, w1_ref[:, k * D:(k + 1) * D], preferred_element_type=F32)
        acc = acc + _bdot(jnp.square(jnp.maximum(a, 0.0)), w2_ref[k * D:(k + 1) * D, :])
    xn = x + gt * acc
    if final:
        xn = _rms(xn, gf_ref[...])
    xo_ref[...] = xn


def _mlp(x, mod, mod_row_block, l, p, g_final, final):
    if x.ndim == 3:
        nb, seq, _ = x.shape
        tm = min(MLP_ROWS, seq)
        assert seq % tm == 0
        per = seq // tm
        grid = (nb * per,)
        tile = pl.BlockSpec((None, tm, D), lambda i: (i // per, i % per, 0))
        mod_rows = nb
    else:
        per = 0
        grid = (1,)
        tile = pl.BlockSpec(x.shape, lambda i: (0, 0))
        mod_rows = x.shape[0]
    return pl.pallas_call(
        functools.partial(_mlp_kernel, final=final, tiles_per_mod=per),
        grid=grid,
        in_specs=[tile] + _mod_specs(l, mod_rows, mod_row_block, (3, 4, 5)) + [
            _layer_spec(p['g2'], l), _layer_spec(p['w_ff1'], l), _layer_spec(p['w_ff2'], l),
            pl.BlockSpec((1, D), lambda i: (0, 0))],
        out_specs=tile,
        out_shape=jax.ShapeDtypeStruct(x.shape, F32),
        compiler_params=pltpu.CompilerParams(
            dimension_semantics=("arbitrary",), vmem_limit_bytes=VMEM_LIMIT),
        name="mlp_final" if final else "mlp",
    )(x, mod, mod, mod, p['g2'], p['w_ff1'], p['w_ff2'], g_final)


def kernel(x_prompt, x_sample, c_prompt, c_sample, state_ssm_re, state_ssm_im, w_ada, b_ada, g_norm1, g_norm2, w_in, g_v, w_spatial, b_spatial, lam_re, lam_im, log_dt, b_re, b_im, c_re, c_im, d_skip, w_glu, b_glu, w_out, w_ff1, w_ff2, g_final):
    nbp, seq, _ = x_prompt.shape
    nbs = x_sample.shape[0]
    assert x_sample.shape[1] == 1 and nbs % nbp == 0

    a_re, a_im, bb_re, bb_im = _s5_prep(lam_re, lam_im, log_dt, b_re, b_im)
    wb, wc, wt = _s5_chunk_maps(*_s5_slab_layout(a_re, a_im, bb_re, bb_im, c_re, c_im))
    mod = _ada(jnp.concatenate([c_sample, c_prompt], axis=0), w_ada, b_ada)
    prompt_row_block = nbs // nbp

    vec = lambda a: a.reshape(DEPTH, 1, D)
    p = dict(
        g1=vec(g_norm1), g2=vec(g_norm2), g_v=vec(g_v), d_skip=vec(d_skip), b_glu=vec(b_glu),
        w_in=w_in.astype(BF16), w_glu=w_glu.astype(BF16), w_out=w_out.astype(BF16),
        w_ff1=w_ff1.astype(BF16), w_ff2=w_ff2.astype(BF16),
        w_sp=w_spatial, b_sp=jnp.repeat(jnp.swapaxes(b_spatial, 1, 2), HEAD_DIM, axis=2),
        w_diag=vec(jnp.repeat(w_spatial[:, :, 0, 0], HEAD_DIM, axis=1)),
        b_diag=vec(jnp.repeat(b_spatial[:, :, 0], HEAD_DIM, axis=1)),
        a_re=a_re.reshape(DEPTH, 1, S_HALF), a_im=a_im.reshape(DEPTH, 1, S_HALF), wb=wb, wc=wc, wt=wt)
    gf = g_final.reshape(1, D)
    lane_major = lambda s: jnp.transpose(s, (0, 2, 3, 1)).reshape(DEPTH, S_HALF, nbs)
    h0_re, h0_im = lane_major(state_ssm_re), lane_major(state_ssm_im)

    xp = x_prompt
    xs = x_sample.reshape(nbs, D)
    re_p, im_p, re_s, im_s, v_s = [], [], [], [], []
    for l in range(DEPTH):
        final = l == DEPTH - 1
        xp, hr, hi = _mix_prompt(xp, mod, prompt_row_block, l, p)
        xp = _mlp(xp, mod, prompt_row_block, l, p, gf, final)
        re_p.append(hr)
        im_p.append(hi)
        xs, hr, hi, v = _mix_sample(xs, mod, l, h0_re, h0_im, p)
        xs = _mlp(xs, mod, 0, l, p, gf, final)
        re_s.append(hr)
        im_s.append(hi)
        v_s.append(v)

    state = lambda hs: jnp.stack(hs).reshape(DEPTH, nbp, GROUPS, STATE)
    state_t = lambda hs: jnp.transpose(jnp.stack(hs).reshape(DEPTH, GROUPS, STATE, nbs), (0, 3, 1, 2))
    return (xp, xs.reshape(nbs, 1, D), state(re_p), state(im_p),
            state_t(re_s), state_t(im_s), jnp.stack(v_s).reshape(DEPTH, nbs, 1, D))
```
